```python
import jax
import jax.numpy as jnp
from jax import lax
import numpy as np

D_MODEL = 1024
BATCH = 8
SEQ = 8192
DEPTH = 1
DEC_BATCH = 32
DEC_SEQ = 2048
PAST_LEN = 128

N_META = 16
HEAD_DIM = 64
N_HEADS = 8
N_KV_HEADS = 2
Q_PER_KV = N_HEADS // N_KV_HEADS
ATTN_W = N_HEADS * HEAD_DIM
KV_W = N_KV_HEADS * HEAD_DIM
REC_W = D_MODEL // 2
REC_BLOCKS = 8
REC_BW = REC_W // REC_BLOCKS
MIX_W = ATTN_W + REC_W
IN_COLS = ATTN_W + 2 * KV_W + 2 * REC_W
CONV_W = 4
CONV_LEFT = 2
RG_C = 8.0
WINDOW = 128
ATTN_BLK = 128
ROT_DIM = HEAD_DIM // 4
ROPE_THETA = 500000.0
N_GROUPS = 4
EXP_PER_GROUP = 8
N_EXPERTS = N_GROUPS * EXP_PER_GROUP
TOP_K = 2
D_EXPERT = D_MODEL // 2
MOE_BLK = 256
EPS = 1e-6

kernel_name = 'hymba_rglru_swa_hier_moe_encoder'


def rmsnorm(x, g):
    x32 = x.astype(jnp.float32)
    y = x32 * lax.rsqrt(jnp.mean(x32 * x32, axis=-1, keepdims=True) + EPS)
    return (y * g.astype(jnp.float32)).astype(x.dtype)


def rope_tables(t):
    inv = ROPE_THETA ** (-jnp.arange(0, ROT_DIM, 2, dtype=jnp.float32) / ROT_DIM)
    ang = jnp.arange(t, dtype=jnp.float32)[:, None] * inv[None, :]
    return jnp.cos(ang), jnp.sin(ang)


def apply_partial_rope(x, cos, sin):
    half = ROT_DIM // 2
    c = cos[None, :, None, :]
    s = sin[None, :, None, :]
    x1 = x[..., :half]
    x2 = x[..., half:ROT_DIM]
    return jnp.concatenate([x1 * c - x2 * s, x2 * c + x1 * s, x[..., ROT_DIM:]], axis=-1)


def sink_softmax_av(s_win, v_win, s_meta, v_meta, sink):
    sk = sink[None, :, :, None, None]
    m = jnp.maximum(jnp.maximum(s_win.max(-1, keepdims=True), s_meta.max(-1, keepdims=True)), sk)
    p_win = jnp.exp(s_win - m)
    p_meta = jnp.exp(s_meta - m)
    denom = p_win.sum(-1, keepdims=True) + p_meta.sum(-1, keepdims=True) + jnp.exp(sk - m)
    out = jnp.einsum('bgrqk,bkgd->bgrqd', p_win, v_win) + jnp.einsum('bgrqk,bkgd->bgrqd', p_meta, v_meta)
    return out / denom


def windowed_gqa(q, k, v, sink):
    b, t = q.shape[0], q.shape[1]
    s_len = t - N_META
    nb = s_len // ATTN_BLK
    q = q.reshape(b, t, N_KV_HEADS, Q_PER_KV, HEAD_DIM) * (HEAD_DIM ** -0.5)
    sk = sink.astype(jnp.float32).reshape(N_KV_HEADS, Q_PER_KV)
    qm, qr = q[:, :N_META], q[:, N_META:]
    km, kr = k[:, :N_META], k[:, N_META:]
    vm, vr = v[:, :N_META], v[:, N_META:]
    pad = ((0, 0), (ATTN_BLK, ATTN_BLK), (0, 0), (0, 0))
    kpad = jnp.pad(kr, pad)
    vpad = jnp.pad(vr, pad)
    qq = jnp.arange(ATTN_BLK)[:, None]
    kk = jnp.arange(3 * ATTN_BLK)[None, :]

    def band(n):
        qb = lax.dynamic_slice_in_dim(qr, n * ATTN_BLK, ATTN_BLK, axis=1)
        kb = lax.dynamic_slice_in_dim(kpad, n * ATTN_BLK, 3 * ATTN_BLK, axis=1)
        vb = lax.dynamic_slice_in_dim(vpad, n * ATTN_BLK, 3 * ATTN_BLK, axis=1)
        qi = n * ATTN_BLK + qq
        ki = (n - 1) * ATTN_BLK + kk
        valid = (jnp.abs(ki - qi) <= WINDOW) & (ki >= 0) & (ki < s_len)
        s_win = jnp.where(valid, jnp.einsum('bqgrd,bkgd->bgrqk', qb, kb), -jnp.inf)
        s_met = jnp.einsum('bqgrd,bkgd->bgrqk', qb, km)
        return sink_softmax_av(s_win, vb, s_met, vm, sk)

    out_r = lax.map(band, jnp.arange(nb))
    out_r = out_r.transpose(1, 0, 4, 2, 3, 5).reshape(b, s_len, ATTN_W)
    qpos = jnp.arange(N_META)[:, None]
    kpos = N_META + jnp.arange(ATTN_BLK)[None, :]
    s_mr = jnp.where(kpos - qpos <= WINDOW,
                     jnp.einsum('bqgrd,bkgd->bgrqk', qm, kr[:, :ATTN_BLK]), -jnp.inf)
    s_mm = jnp.einsum('bqgrd,bkgd->bgrqk', qm, km)
    out_m = sink_softmax_av(s_mr, vr[:, :ATTN_BLK], s_mm, vm, sk)
    out_m = out_m.transpose(0, 3, 1, 2, 4).reshape(b, N_META, ATTN_W)
    return jnp.concatenate([out_m, out_r], axis=1)


def _lin_combine(c1, c2):
    a1, b1 = c1
    a2, b2 = c2
    return a1 * a2, a2 * b1 + b2


def centred_depthwise_conv(x, w, b):
    t = x.shape[1]
    xp = jnp.pad(x, ((0, 0), (CONV_LEFT, CONV_W - 1 - CONV_LEFT), (0, 0)))
    return sum(xp[:, j:j + t] * w[j] for j in range(CONV_W)) + b


def rglru_scan(xc, w_r, b_r, w_i, b_i, lam, reverse):
    b, t, c = xc.shape
    xblk = xc.reshape(b, t, REC_BLOCKS, REC_BW)
    gate_r = jax.nn.sigmoid(jnp.einsum('btni,nij->btnj', xblk, w_r).reshape(b, t, c) + b_r)
    gate_i = jax.nn.sigmoid(jnp.einsum('btni,nij->btnj', xblk, w_i).reshape(b, t, c) + b_i)
    log_a = -RG_C * gate_r * jax.nn.softplus(-lam)
    a = jnp.exp(log_a)
    u = jnp.sqrt(-jnp.expm1(2.0 * log_a)) * (gate_i * xc)
    if reverse:
        a = jnp.flip(a, axis=1)
        u = jnp.flip(u, axis=1)
    _, h = lax.associative_scan(_lin_combine, (a, u), axis=1)
    return jnp.flip(h, axis=1) if reverse else h


def mixer(h, cos, sin, p):
    f32 = jnp.float32
    b, t, _ = h.shape
    proj = h @ p['w_in']
    q, k, v, xr, yg = jnp.split(
        proj, [ATTN_W, ATTN_W + KV_W, ATTN_W + 2 * KV_W, ATTN_W + 2 * KV_W + REC_W], axis=-1)
    q = apply_partial_rope(q.astype(f32).reshape(b, t, N_HEADS, HEAD_DIM), cos, sin)
    k = apply_partial_rope(k.astype(f32).reshape(b, t, N_KV_HEADS, HEAD_DIM), cos, sin)
    v = v.astype(f32).reshape(b, t, N_KV_HEADS, HEAD_DIM)
    attn = windowed_gqa(q, k, v, p['attn_sink'])
    xc = centred_depthwise_conv(xr.astype(f32), p['conv_w'].astype(f32), p['conv_b'].astype(f32))
    rec = (rglru_scan(xc, p['w_rgate_fwd'].astype(f32), p['b_rgate_fwd'].astype(f32),
                      p['w_igate_fwd'].astype(f32), p['b_igate_fwd'].astype(f32),
                      p['lam_fwd'].astype(f32), False)
           + rglru_scan(xc, p['w_rgate_bwd'].astype(f32), p['b_rgate_bwd'].astype(f32),
                        p['w_igate_bwd'].astype(f32), p['b_igate_bwd'].astype(f32),
                        p['lam_bwd'].astype(f32), True))
    rec = rec * jax.nn.gelu(yg.astype(f32))
    mixed = jnp.concatenate([rmsnorm(attn, p['g_attn_out']), rmsnorm(rec, p['g_rec_out'])], axis=-1)
    return mixed.astype(h.dtype) @ p['w_out']


def expert_dispatch(u, eid, gates, w_g, w_u, w_d):
    n, d = u.shape
    a_len = n * TOP_K
    rows = -(-(a_len + N_EXPERTS * (MOE_BLK - 1)) // MOE_BLK) * MOE_BLK
    nblk = rows // MOE_BLK
    flat_e = eid.reshape(-1)
    flat_tok = jnp.repeat(jnp.arange(n, dtype=jnp.int32), TOP_K)
    flat_g = gates.reshape(-1)
    order = jnp.argsort(flat_e)
    se = flat_e[order]
    counts = jnp.zeros((N_EXPERTS,), jnp.int32).at[flat_e].add(1)
    padded = (counts + MOE_BLK - 1) // MOE_BLK * MOE_BLK
    start = jnp.cumsum(counts) - counts
    pend = jnp.cumsum(padded)
    pstart = pend - padded
    dest = pstart[se] + (jnp.arange(a_len, dtype=jnp.int32) - start[se])
    row_tok = jnp.full((rows,), n, jnp.int32).at[dest].set(flat_tok[order])
    row_gate = jnp.zeros((rows,), jnp.float32).at[dest].set(flat_g[order])
    blk_e = jnp.minimum(
        jnp.searchsorted(pend, jnp.arange(nblk, dtype=jnp.int32) * MOE_BLK, side='right'),
        N_EXPERTS - 1)
    u_pad = jnp.concatenate([u, jnp.zeros((1, d), u.dtype)], axis=0)

    def run_block(args):
        tok, e = args
        xb = u_pad[tok]
        hid = jax.nn.silu(xb @ w_g[e]) * (xb @ w_u[e])
        return hid @ w_d[e]

    yb = lax.map(run_block, (row_tok.reshape(nblk, MOE_BLK), blk_e)).reshape(rows, d)
    yb = yb * row_gate[:, None].astype(yb.dtype)
    return jnp.zeros((n + 1, d), yb.dtype).at[row_tok].add(yb)[:n]


def hier_moe(u, p):
    f32 = jnp.float32
    b, t, d = u.shape
    n = b * t
    uf = u.reshape(n, d)
    grp_logits = (uf @ p['w_router_grp']).astype(f32) + p['b_router_grp'].astype(f32)
    grp = jnp.argmax(grp_logits, axis=-1).astype(jnp.int32)
    p_grp = jnp.take_along_axis(jax.nn.softmax(grp_logits, axis=-1), grp[:, None], axis=-1)
    exp_logits = (uf @ p['w_router_exp']).astype(f32) + p['b_router_exp'].astype(f32)
    idx = grp[:, None] * EXP_PER_GROUP + jnp.arange(EXP_PER_GROUP, dtype=jnp.int32)[None, :]
    in_grp = jnp.take_along_axis(exp_logits, idx, axis=-1)
    top_v, top_i = lax.top_k(in_grp, TOP_K)
    gates = jax.nn.softmax(top_v, axis=-1) * p_grp
    eid = grp[:, None] * EXP_PER_GROUP + top_i.astype(jnp.int32)
    y = expert_dispatch(uf, eid, gates, p['w_exp_gate'], p['w_exp_up'], p['w_exp_down'])
    return y.reshape(b, t, d)


def encode(x, meta_tokens, layers, ln_final_g):
    b = x.shape[0]
    meta = jnp.broadcast_to(meta_tokens.astype(x.dtype)[None], (b, N_META, D_MODEL))
    h = jnp.concatenate([meta, x], axis=1)
    cos, sin = rope_tables(h.shape[1])
    for l in range(DEPTH):
        p = {name: arr[l] for name, arr in layers.items()}
        h = h + mixer(rmsnorm(h, p['ln_mix_g']), cos, sin, p)
        h = h + hier_moe(rmsnorm(h, p['ln_ffn_g']), p)
    return rmsnorm(h, ln_final_g)[:, N_META:]


def setup_inputs(seed: int = 0) -> dict:
    key = jax.random.key(seed)
    ks = jax.random.split(key, 32)
    f32 = jnp.float32

    def nrm(k, shape, scale):
        return jax.random.normal(k, shape, f32) * scale

    def lam_init(k):
        a_c = jax.random.uniform(k, (DEPTH, REC_W), f32, 0.9, 0.999)
        a0 = a_c ** (1.0 / RG_C)
        return jnp.log(a0) - jnp.log1p(-a0)

    return {
        'x_prompt': nrm(ks[0], (BATCH, SEQ, D_MODEL), 1.0),
        'x_sample': nrm(ks[1], (DEC_BATCH, DEC_SEQ, D_MODEL), 1.0),
        'meta_tokens': nrm(ks[2], (N_META, D_MODEL), 1.0),
        'ln_mix_g': 1.0 + nrm(ks[3], (DEPTH, D_MODEL), 0.02),
        'w_in': nrm(ks[4], (DEPTH, D_MODEL, IN_COLS), D_MODEL ** -0.5),
        'conv_w': nrm(ks[5], (DEPTH, CONV_W, REC_W), CONV_W ** -0.5),
        'conv_b': nrm(ks[6], (DEPTH, REC_W), 0.01),
        'w_rgate_fwd': nrm(ks[7], (DEPTH, REC_BLOCKS, REC_BW, REC_BW), REC_BW ** -0.5),
        'b_rgate_fwd': nrm(ks[8], (DEPTH, REC_W), 0.01),
        'w_igate_fwd': nrm(ks[9], (DEPTH, REC_BLOCKS, REC_BW, REC_BW), REC_BW ** -0.5),
        'b_igate_fwd': nrm(ks[10], (DEPTH, REC_W), 0.01),
        'lam_fwd': lam_init(ks[11]),
        'w_rgate_bwd': nrm(ks[12], (DEPTH, REC_BLOCKS, REC_BW, REC_BW), REC_BW ** -0.5),
        'b_rgate_bwd': nrm(ks[13], (DEPTH, REC_W), 0.01),
        'w_igate_bwd': nrm(ks[14], (DEPTH, REC_BLOCKS, REC_BW, REC_BW), REC_BW ** -0.5),
        'b_igate_bwd': nrm(ks[15], (DEPTH, REC_W), 0.01),
        'lam_bwd': lam_init(ks[16]),
        'attn_sink': nrm(ks[17], (DEPTH, N_HEADS), 1.0),
        'g_attn_out': 1.0 + nrm(ks[18], (DEPTH, ATTN_W), 0.02),
        'g_rec_out': 1.0 + nrm(ks[19], (DEPTH, REC_W), 0.02),
        'w_out': nrm(ks[20], (DEPTH, MIX_W, D_MODEL), MIX_W ** -0.5),
        'ln_ffn_g': 1.0 + nrm(ks[21], (DEPTH, D_MODEL), 0.02),
        'w_router_grp': nrm(ks[22], (DEPTH, D_MODEL, N_GROUPS), D_MODEL ** -0.5),
        'b_router_grp': nrm(ks[23], (DEPTH, N_GROUPS), 0.01),
        'w_router_exp': nrm(ks[24], (DEPTH, D_MODEL, N_EXPERTS), D_MODEL ** -0.5),
        'b_router_exp': nrm(ks[25], (DEPTH, N_EXPERTS), 0.01),
        'w_exp_gate': nrm(ks[26], (DEPTH, N_EXPERTS, D_MODEL, D_EXPERT), D_MODEL ** -0.5),
        'w_exp_up': nrm(ks[27], (DEPTH, N_EXPERTS, D_MODEL, D_EXPERT), D_MODEL ** -0.5),
        'w_exp_down': nrm(ks[28], (DEPTH, N_EXPERTS, D_EXPERT, D_MODEL), D_EXPERT ** -0.5),
        'ln_final_g': 1.0 + nrm(ks[29], (D_MODEL,), 0.02),
    }


def reference(x_prompt, x_sample, meta_tokens, ln_mix_g, w_in, conv_w, conv_b,
              w_rgate_fwd, b_rgate_fwd, w_igate_fwd, b_igate_fwd, lam_fwd,
              w_rgate_bwd, b_rgate_bwd, w_igate_bwd, b_igate_bwd, lam_bwd,
              attn_sink, g_attn_out, g_rec_out, w_out, ln_ffn_g,
              w_router_grp, b_router_grp, w_router_exp, b_router_exp,
              w_exp_gate, w_exp_up, w_exp_down, ln_final_g):
    layers = {
        'ln_mix_g': ln_mix_g, 'w_in': w_in, 'conv_w': conv_w, 'conv_b': conv_b,
        'w_rgate_fwd': w_rgate_fwd, 'b_rgate_fwd': b_rgate_fwd,
        'w_igate_fwd': w_igate_fwd, 'b_igate_fwd': b_igate_fwd, 'lam_fwd': lam_fwd,
        'w_rgate_bwd': w_rgate_bwd, 'b_rgate_bwd': b_rgate_bwd,
        'w_igate_bwd': w_igate_bwd, 'b_igate_bwd': b_igate_bwd, 'lam_bwd': lam_bwd,
        'attn_sink': attn_sink, 'g_attn_out': g_attn_out, 'g_rec_out': g_rec_out,
        'w_out': w_out, 'ln_ffn_g': ln_ffn_g,
        'w_router_grp': w_router_grp, 'b_router_grp': b_router_grp,
        'w_router_exp': w_router_exp, 'b_router_exp': b_router_exp,
        'w_exp_gate': w_exp_gate, 'w_exp_up': w_exp_up, 'w_exp_down': w_exp_down,
    }
    y_prompt = encode(x_prompt, meta_tokens, layers, ln_final_g)
    y_sample = encode(x_sample, meta_tokens, layers, ln_final_g)
    return (y_prompt, y_sample)
```

```python
import functools

import numpy as np
import jax
import jax.numpy as jnp
from jax import lax
from jax.experimental import pallas as pl
from jax.experimental.pallas import tpu as pltpu

D_MODEL = 1024
N_META = 16
HEAD_DIM = 64
N_HEADS = 8
N_KV_HEADS = 2
Q_PER_KV = N_HEADS // N_KV_HEADS
ATTN_W = N_HEADS * HEAD_DIM
KV_W = N_KV_HEADS * HEAD_DIM
REC_W = D_MODEL // 2
REC_BLOCKS = 8
REC_BW = REC_W // REC_BLOCKS
IN_COLS = ATTN_W + 2 * KV_W + 2 * REC_W
CONV_W = 4
CONV_LEFT = 2
RG_C = 8.0
WINDOW = 128
ATTN_BLK = 128
ROT_DIM = HEAD_DIM // 4
ROPE_THETA = 500000.0
N_GROUPS = 4
EXP_PER_GROUP = 8
N_EXPERTS = N_GROUPS * EXP_PER_GROUP
D_EXPERT = D_MODEL // 2
EPS = 1e-6

LANES = 128
SUBLANES = 8
PAIRS_PER_GROUP = EXP_PER_GROUP * (EXP_PER_GROUP - 1) // 2
N_CLASSES = N_GROUPS * PAIRS_PER_GROUP
ROW_W = D_MODEL + LANES
NEG_BIG = -1e30
VMEM_LIMIT = 48 * 1024 * 1024

F32 = jnp.float32
BF16 = jnp.bfloat16


def _rms(x, g):
    ms = jnp.mean(x * x, axis=-1, keepdims=True)
    return x * lax.rsqrt(ms + EPS) * g


def _sigmoid(x):
    return 0.5 * jnp.tanh(0.5 * x) + 0.5


def _params(*sem):
    return pltpu.CompilerParams(dimension_semantics=sem, vmem_limit_bytes=VMEM_LIMIT)


def _proj_kernel(x_ref, g_ref, w_ref, c_ref, sa_ref, sb_ref, q_ref, kk_ref, vv_ref, xr_ref, yg_ref):
    xn = _rms(x_ref[0], g_ref[...])
    p = jnp.dot(xn.astype(BF16), w_ref[...], preferred_element_type=F32)
    c, sa, sb = c_ref[...], sa_ref[...], sb_ref[...]

    def rope(t):
        return t * c + pltpu.roll(t, LANES - ROT_DIM // 2, 1) * sa + pltpu.roll(t, ROT_DIM // 2, 1) * sb

    for j in range(ATTN_W // LANES):
        qj = rope(p[:, j * LANES:(j + 1) * LANES]) * (HEAD_DIM ** -0.5)
        q_ref[0, :, j * LANES:(j + 1) * LANES] = qj.astype(BF16)
    k = rope(p[:, ATTN_W:ATTN_W + KV_W])
    v = p[:, ATTN_W + KV_W:ATTN_W + 2 * KV_W]
    kk_ref[0, :, :KV_W] = k.astype(BF16)
    kk_ref[0, :, KV_W:] = pltpu.roll(k, HEAD_DIM, 1).astype(BF16)
    vv_ref[0, :, :KV_W] = v.astype(BF16)
    vv_ref[0, :, KV_W:] = pltpu.roll(v, HEAD_DIM, 1).astype(BF16)
    o = ATTN_W + 2 * KV_W
    xr_ref[0] = p[:, o:o + REC_W]
    yg_ref[0] = p[:, o + REC_W:o + 2 * REC_W].astype(BF16)


def _proj(x, g, w_in, tabs, tm):
    b, s, _ = x.shape
    tab_spec = pl.BlockSpec((tm, LANES), lambda i, j: (j, 0))
    row = lambda w: pl.BlockSpec((1, tm, w), lambda i, j: (i, j, 0))
    return pl.pallas_call(
        _proj_kernel,
        grid=(b, s // tm),
        in_specs=[row(D_MODEL),
                  pl.BlockSpec((1, D_MODEL), lambda i, j: (0, 0)),
                  pl.BlockSpec((D_MODEL, IN_COLS), lambda i, j: (0, 0)),
                  tab_spec, tab_spec, tab_spec],
        out_specs=[row(ATTN_W), row(2 * KV_W), row(2 * KV_W), row(REC_W), row(REC_W)],
        out_shape=[jax.ShapeDtypeStruct((b, s, ATTN_W), BF16),
                   jax.ShapeDtypeStruct((b, s, 2 * KV_W), BF16),
                   jax.ShapeDtypeStruct((b, s, 2 * KV_W), BF16),
                   jax.ShapeDtypeStruct((b, s, REC_W), F32),
                   jax.ShapeDtypeStruct((b, s, REC_W), BF16)],
        compiler_params=_params("parallel", "parallel"),
        name="proj",
    )(x, g, w_in, *tabs)


def _rope_tabs(pos0, n):
    half = ROT_DIM // 2
    inv = ROPE_THETA ** (-jnp.arange(0, ROT_DIM, 2, dtype=F32) / ROT_DIM)
    ang = jnp.arange(pos0 + n, dtype=F32)[pos0:, None] * inv[None, :]
    cos, sin = jnp.cos(ang), jnp.sin(ang)
    one = jnp.ones((n, HEAD_DIM - ROT_DIM), F32)
    zero = jnp.zeros((n, HEAD_DIM - ROT_DIM), F32)
    zh = jnp.zeros((n, half), F32)
    c = jnp.concatenate([cos, cos, one], axis=1)
    sa = jnp.concatenate([-sin, zh, zero], axis=1)
    sb = jnp.concatenate([zh, sin, zero], axis=1)
    return tuple(jnp.concatenate([t, t], axis=1) for t in (c, sa, sb))


def _attn_kernel(q_ref, kp_ref, kc_ref, kn_ref, km_ref, vp_ref, vc_ref, vn_ref, vm_ref,
                 sink_ref, g_ref, o_ref):
    n = pl.program_id(1)
    nb = pl.num_programs(1)
    blk = ATTN_BLK
    kall = jnp.concatenate([kp_ref[0], kc_ref[0], kn_ref[0], km_ref[...]], axis=0)
    vall = jnp.concatenate([vp_ref[0], vc_ref[0], vn_ref[0], vm_ref[...]], axis=0)
    nk = kall.shape[0]
    r = lax.broadcasted_iota(jnp.int32, (blk, nk), 0)
    c = lax.broadcasted_iota(jnp.int32, (blk, nk), 1)
    off_prev = jnp.where(n > 0, 0, blk)
    off_next = jnp.where(n < nb - 1, 0, blk)
    valid = (((c < blk) & (c >= r + off_prev)) | ((c >= blk) & (c < 2 * blk))
             | ((c >= 2 * blk) & (c < 3 * blk) & (c - 2 * blk <= r - off_next)) | (c >= 3 * blk))
    lane = lax.broadcasted_iota(jnp.int32, (blk, LANES), 1)
    low = lane < HEAD_DIM
    cols = []
    for j in range(ATTN_W // LANES):
        qj = q_ref[0, :, j * LANES:(j + 1) * LANES]
        g = (2 * j) // Q_PER_KV
        outs = []
        for half in range(2):
            h = 2 * j + half
            qh = jnp.where(low if half == 0 else ~low, qj, jnp.zeros_like(qj))
            sel = slice(0, KV_W) if half == g else slice(KV_W, 2 * KV_W)
            s = lax.dot_general(qh, kall[:, sel], (((1,), (1,)), ((), ())), preferred_element_type=F32)
            s = jnp.where(valid, s, NEG_BIG)
            sk = sink_ref[:, h:h + 1]
            m = jnp.maximum(jnp.max(s, axis=-1, keepdims=True), sk)
            p = jnp.exp(s - m)
            den = jnp.sum(p, axis=-1, keepdims=True) + jnp.exp(sk - m)
            pv = jnp.dot(p.astype(BF16), vall[:, sel], preferred_element_type=F32)
            outs.append(pv / den)
        cols.append(jnp.where(low, outs[0], outs[1]))
    attn = jnp.concatenate(cols, axis=1)
    o_ref[0] = _rms(attn, g_ref[...]).astype(BF16)


def _attn(q, kk, vv, kkm, vvm, sink, g):
    b, s, _ = q.shape
    nb = s // ATTN_BLK
    prev = lambda i, n: (i, jnp.maximum(n - 1, 0), 0)
    cur = lambda i, n: (i, n, 0)
    nxt = lambda i, n: (i, jnp.minimum(n + 1, nb - 1), 0)
    kvs = lambda f: pl.BlockSpec((1, ATTN_BLK, 2 * KV_W), f)
    meta = pl.BlockSpec((N_META, 2 * KV_W), lambda i, n: (0, 0))
    return pl.pallas_call(
        _attn_kernel,
        grid=(b, nb),
        in_specs=[pl.BlockSpec((1, ATTN_BLK, ATTN_W), cur),
                  kvs(prev), kvs(cur), kvs(nxt), meta,
                  kvs(prev), kvs(cur), kvs(nxt), meta,
                  pl.BlockSpec((1, N_HEADS), lambda i, n: (0, 0)),
                  pl.BlockSpec((1, ATTN_W), lambda i, n: (0, 0))],
        out_specs=pl.BlockSpec((1, ATTN_BLK, ATTN_W), cur),
        out_shape=jax.ShapeDtypeStruct((b, s, ATTN_W), BF16),
        compiler_params=_params("parallel", "parallel"),
        name="attn",
    )(q, kk, kk, kk, kkm, vv, vv, vv, vvm, sink, g)


def _scan_rows(a, u, carry, reverse):
    rows, width = a.shape
    row = lax.broadcasted_iota(jnp.int32, (SUBLANES, width), 0)
    out = [None] * (rows // SUBLANES)
    order = range(rows // SUBLANES)
    for sidx in (reversed(order) if reverse else order):
        a8 = a[sidx * SUBLANES:(sidx + 1) * SUBLANES]
        u8 = u[sidx * SUBLANES:(sidx + 1) * SUBLANES]
        d = 1
        while d < SUBLANES:
            shift = SUBLANES - d if reverse else d
            keep = (row < SUBLANES - d) if reverse else (row >= d)
            a_s = pltpu.roll(a8, shift, 0)
            u_s = pltpu.roll(u8, shift, 0)
            u8 = jnp.where(keep, a8 * u_s + u8, u8)
            a8 = jnp.where(keep, a8 * a_s, a8)
            d *= 2
        h8 = a8 * carry + u8
        carry = h8[0:1] if reverse else h8[SUBLANES - 1:SUBLANES]
        out[sidx] = h8
    return jnp.concatenate(out, axis=0), carry


def _gate_scan(xc, w_ref, br, bi, nsp, carry, reverse):
    hs, cs = [], []
    half = REC_W // 2
    for ch in range(2):
        sl = slice(ch * half, (ch + 1) * half)
        xcc = xc[:, sl]
        pre = jnp.dot(xcc.astype(BF16), w_ref[ch], preferred_element_type=F32)
        gate_r = _sigmoid(pre[:, :half] + br[:, sl])
        gate_i = _sigmoid(pre[:, half:] + bi[:, sl])
        log_a = nsp[:, sl] * gate_r
        a = jnp.exp(log_a)
        mult = jnp.sqrt(-jnp.tanh(log_a) * (1.0 + a * a))
        u = mult * (gate_i * xcc)
        h, c = _scan_rows(a, u, carry[:, sl], reverse)
        hs.append(h)
        cs.append(c)
    return jnp.concatenate(hs, axis=1), jnp.concatenate(cs, axis=1)


def _rec_kernel(xp_ref, x_ref, xn_ref, xm_ref, cw_ref, cb_ref, w_ref, br_ref, bi_ref, nsp_ref,
                h_ref, xs_ref, carry_ref, *, tm, reverse):
    j = pl.program_id(1)
    nt = pl.num_programs(1)
    first_tile = (j == nt - 1) if reverse else (j == 0)
    last_tile = (j == 0) if reverse else (j == nt - 1)
    cw = cw_ref[...]
    cb = cb_ref[...]
    br, bi, nsp = br_ref[...], bi_ref[...], nsp_ref[...]
    halo = SUBLANES

    def conv(rows):
        acc = cb + cw[0:1] * xs_ref[halo - CONV_LEFT:halo - CONV_LEFT + rows]
        for t in range(1, CONV_W):
            acc = acc + cw[t:t + 1] * xs_ref[halo - CONV_LEFT + t:halo - CONV_LEFT + t + rows]
        return acc

    if reverse:
        @pl.when(j == 0)
        def _():
            carry_ref[...] = jnp.zeros_like(carry_ref)
    else:
        @pl.when(j == 0)
        def _():
            xs_ref[0:halo] = jnp.zeros((halo, REC_W), F32)
            xs_ref[halo:halo + N_META] = xm_ref[...]
            xs_ref[halo + N_META:2 * halo + N_META] = x_ref[0, 0:halo]
            _, c = _gate_scan(conv(N_META), w_ref, br, bi, nsp, jnp.zeros((1, REC_W), F32), False)
            carry_ref[0:1] = c

    xs_ref[0:halo] = jnp.where(first_tile, xm_ref[N_META - halo:N_META], xp_ref[0])
    xs_ref[halo:halo + tm] = x_ref[0]
    xs_ref[halo + tm:2 * halo + tm] = jnp.where(last_tile, jnp.zeros((halo, REC_W), F32), xn_ref[0])
    h, c = _gate_scan(conv(tm), w_ref, br, bi, nsp, carry_ref[0:1], reverse)
    h_ref[0] = h
    carry_ref[0:1] = c


def _rec(xr, xr_meta, cw, cb, wcat, br, bi, nsp, tm, reverse):
    b, s, _ = xr.shape
    nt = s // tm
    per = tm // SUBLANES
    nh = s // SUBLANES
    t_of = (lambda j: nt - 1 - j) if reverse else (lambda j: j)
    vec = lambda r: pl.BlockSpec((r, REC_W), lambda i, j: (0, 0))
    return pl.pallas_call(
        functools.partial(_rec_kernel, tm=tm, reverse=reverse),
        grid=(b, nt),
        in_specs=[pl.BlockSpec((1, SUBLANES, REC_W), lambda i, j: (i, jnp.maximum(t_of(j) * per - 1, 0), 0)),
                  pl.BlockSpec((1, tm, REC_W), lambda i, j: (i, t_of(j), 0)),
                  pl.BlockSpec((1, SUBLANES, REC_W),
                               lambda i, j: (i, jnp.minimum((t_of(j) + 1) * per, nh - 1), 0)),
                  vec(N_META), vec(CONV_W), vec(1),
                  pl.BlockSpec((2, REC_W // 2, REC_W), lambda i, j: (0, 0, 0)),
                  vec(1), vec(1), vec(1)],
        out_specs=pl.BlockSpec((1, tm, REC_W), lambda i, j: (i, t_of(j), 0)),
        out_shape=jax.ShapeDtypeStruct((b, s, REC_W), F32),
        scratch_shapes=[pltpu.VMEM((tm + 2 * SUBLANES, REC_W), F32), pltpu.VMEM((SUBLANES, REC_W), F32)],
        compiler_params=_params("parallel", "arbitrary"),
        name="rec_bwd" if reverse else "rec_fwd",
    )(xr, xr, xr, xr_meta, cw, cb, wcat, br, bi, nsp)


def _gate_weights(w_r, w_i):
    per = REC_BLOCKS // 2

    def bd(w4):
        z = jnp.zeros((per * REC_BW, per * REC_BW), F32)
        for n in range(per):
            z = z.at[n * REC_BW:(n + 1) * REC_BW, n * REC_BW:(n + 1) * REC_BW].set(w4[n])
        return z

    halves = [jnp.concatenate([bd(w_r[c * per:(c + 1) * per]), bd(w_i[c * per:(c + 1) * per])], axis=1)
              for c in range(2)]
    return jnp.stack(halves).astype(BF16)


def _pair_index(a, b):
    return (a * (2 * EXP_PER_GROUP - 1 - a)) // 2 + (b - a - 1)


def _outproj_kernel(at_ref, hf_ref, hb_ref, yg_ref, x_ref, w_ref, grec_ref, gffn_ref, wr_ref, brt_ref, o_ref):
    yg = yg_ref[0].astype(F32)
    gelu = 0.5 * yg * (1.0 + jnp.tanh(np.sqrt(2.0 / np.pi).astype(np.float32) * (yg + 0.044715 * (yg * yg * yg))))
    rec = (hf_ref[0] + hb_ref[0]) * gelu
    mixed = jnp.concatenate([at_ref[0], _rms(rec, grec_ref[...]).astype(BF16)], axis=1)
    h1 = x_ref[0] + jnp.dot(mixed, w_ref[...], preferred_element_type=F32)
    o_ref[0, :, :D_MODEL] = h1

    u = _rms(h1, gffn_ref[...]).astype(BF16)
    lg = jnp.dot(u, wr_ref[...], preferred_element_type=F32) + brt_ref[...]
    lane_i = lax.broadcasted_iota(jnp.int32, lg.shape, 1)
    lane = lane_i.astype(F32)
    lane_grp = (lane_i >> 3).astype(F32)
    ninf = -jnp.inf
    big = jnp.float32(1 << 20)
    is_grp = (lane_i >= N_EXPERTS) & (lane_i < N_EXPERTS + N_GROUPS)
    gl = jnp.where(is_grp, lg, ninf)
    gmax = jnp.max(gl, axis=-1, keepdims=True)
    gidx = jnp.min(jnp.where(gl == gmax, lane - N_EXPERTS, big), axis=-1, keepdims=True)
    p_grp = 1.0 / jnp.sum(jnp.where(is_grp, jnp.exp(gl - gmax), 0.0), axis=-1, keepdims=True)
    in_grp = (lane_i < N_EXPERTS) & (lane_grp == gidx)
    el = jnp.where(in_grp, lg, ninf)
    m1 = jnp.max(el, axis=-1, keepdims=True)
    i1 = jnp.min(jnp.where(el == m1, lane, big), axis=-1, keepdims=True)
    el2 = jnp.where(lane == i1, ninf, el)
    m2 = jnp.max(el2, axis=-1, keepdims=True)
    i2 = jnp.min(jnp.where(el2 == m2, lane, big), axis=-1, keepdims=True)
    e2 = jnp.exp(m2 - m1)
    g1 = p_grp / (1.0 + e2)
    g2 = p_grp * e2 / (1.0 + e2)
    first_lo = i1 < i2
    lo = jnp.where(first_lo, i1, i2) - gidx * EXP_PER_GROUP
    hi = jnp.where(first_lo, i2, i1) - gidx * EXP_PER_GROUP
    cls = gidx * PAIRS_PER_GROUP + 0.5 * (lo * (2 * EXP_PER_GROUP - 1 - lo)) + (hi - lo - 1.0)
    g_lo = jnp.where(first_lo, g1, g2)
    g_hi = jnp.where(first_lo, g2, g1)
    tl = lax.broadcasted_iota(jnp.int32, (lg.shape[0], LANES), 1)
    o_ref[0, :, D_MODEL:] = jnp.where(tl == 0, g_lo, jnp.where(tl == 1, g_hi, jnp.where(tl == 2, cls, 0.0)))


def _outproj(attn_n, hf, hb, yg, x, w_out, g_rec, g_ffn, w_router, b_router, tm):
    b, s, _ = x.shape
    row = lambda w: pl.BlockSpec((1, tm, w), lambda i, j: (i, j, 0))
    full = lambda r, c: pl.BlockSpec((r, c), lambda i, j: (0, 0))
    return pl.pallas_call(
        _outproj_kernel,
        grid=(b, s // tm),
        in_specs=[row(ATTN_W), row(REC_W), row(REC_W), row(REC_W), row(D_MODEL),
                  full(D_MODEL, D_MODEL), full(1, REC_W), full(1, D_MODEL),
                  full(D_MODEL, LANES), full(1, LANES)],
        out_specs=row(ROW_W),
        out_shape=jax.ShapeDtypeStruct((b, s, ROW_W), F32),
        compiler_params=_params("parallel", "parallel"),
        name="outproj",
    )(attn_n, hf, hb, yg, x, w_out, g_rec, g_ffn, w_router, b_router)


def _class_tables():
    lo = np.zeros((N_CLASSES,), np.int32)
    hi = np.zeros((N_CLASSES,), np.int32)
    for g in range(N_GROUPS):
        for a in range(EXP_PER_GROUP):
            for b in range(a + 1, EXP_PER_GROUP):
                c = g * PAIRS_PER_GROUP + _pair_index(a, b)
                lo[c] = g * EXP_PER_GROUP + a
                hi[c] = g * EXP_PER_GROUP + b
    return lo, hi


def _route_plan(cls, bm):
    n = cls.shape[0]
    nblk = n // bm + N_CLASSES
    i32 = jnp.int32
    order = jnp.argsort(cls, stable=True).astype(i32)
    counts = jnp.sum((cls[:, None] == jnp.arange(N_CLASSES, dtype=i32)[None, :]).astype(i32), axis=0)
    start = jnp.cumsum(counts) - counts
    padded = (counts + bm - 1) // bm * bm
    pend = jnp.cumsum(padded)
    pstart = pend - padded
    brow = jnp.arange(nblk, dtype=i32) * bm
    blk_cls = jnp.minimum(jnp.searchsorted(pend, brow, side='right').astype(i32), N_CLASSES - 1)
    nval = jnp.clip(counts[blk_cls] - (brow - pstart[blk_cls]), 0, bm)
    nval = jnp.where(brow < pend[-1], nval, 0).astype(i32)
    idx = jnp.arange(bm, dtype=i32)[None, :] + (brow - pstart[blk_cls] + start[blk_cls])[:, None]
    tok = order[jnp.clip(idx, 0, n - 1)]
    row_tok = jnp.where(jnp.arange(bm, dtype=i32)[None, :] < nval[:, None], tok, 0).astype(i32)
    lo_t, hi_t = _class_tables()
    return row_tok.reshape(nblk, 1, bm), jnp.asarray(lo_t)[blk_cls], jnp.asarray(hi_t)[blk_cls], nval


def _moe_kernel(lo_ref, hi_ref, nval_ref, rt_ref, rtn_ref, h_hbm,
                wg_lo, wu_lo, wd_lo, wg_hi, wu_hi, wd_hi, gffn_ref, gfin_ref,
                out_hbm, xbuf, ybuf, gsem, ssem):
    b = pl.program_id(0)
    nb = pl.num_programs(0)
    slot = b & 1
    nv = nval_ref[b]

    def gather_copy(tok, r, s):
        return pltpu.make_async_copy(h_hbm.at[pl.ds(tok, 1)], xbuf.at[s, pl.ds(r, 1)], gsem.at[s])

    def scatter_copy(tok, r, s):
        return pltpu.make_async_copy(ybuf.at[s, pl.ds(r, 1)], out_hbm.at[pl.ds(tok, 1)], ssem.at[s])

    def gather_start(rt, count, s):
        def body(r, carry):
            gather_copy(rt[0, 0, r], r, s).start()
            return carry
        lax.fori_loop(0, count, body, 0)

    def gather_wait(count, s):
        def body(r, carry):
            gather_copy(0, 0, s).wait()
            return carry
        lax.fori_loop(0, count, body, 0)

    def scatter_start(rt, count, s):
        def body(r, carry):
            scatter_copy(rt[0, 0, r], r, s).start()
            return carry
        lax.fori_loop(0, count, body, 0)

    def scatter_wait(count, s):
        def body(r, carry):
            scatter_copy(0, 0, s).wait()
            return carry
        lax.fori_loop(0, count, body, 0)

    @pl.when(b == 0)
    def _():
        xbuf[...] = jnp.zeros_like(xbuf)
        gather_start(rt_ref, nv, 0)

    gather_wait(nv, slot)

    @pl.when(b + 1 < nb)
    def _():
        gather_start(rtn_ref, nval_ref[jnp.minimum(b + 1, nb - 1)], 1 - slot)

    @pl.when(b >= 2)
    def _():
        scatter_wait(nval_ref[jnp.maximum(b - 2, 0)], slot)

    @pl.when(nv > 0)
    def _():
        x = xbuf[slot]
        h1 = x[:, :D_MODEL]
        g_lo = x[:, D_MODEL:D_MODEL + 1]
        g_hi = x[:, D_MODEL + 1:D_MODEL + 2]
        u = _rms(h1, gffn_ref[...]).astype(BF16)

        def expert(wg, wu, wd):
            a = jnp.dot(u, wg[0], preferred_element_type=F32)
            c = jnp.dot(u, wu[0], preferred_element_type=F32)
            hid = (a * _sigmoid(a)) * c
            return jnp.dot(hid.astype(BF16), wd[0], preferred_element_type=F32)

        y = g_lo * expert(wg_lo, wu_lo, wd_lo) + g_hi * expert(wg_hi, wu_hi, wd_hi)
        ybuf[slot] = _rms(h1 + y, gfin_ref[...])

    scatter_start(rt_ref, nv, slot)

    @pl.when(b == nb - 1)
    def _():
        scatter_wait(nv, slot)

        @pl.when(b >= 1)
        def _():
            scatter_wait(nval_ref[jnp.maximum(b - 1, 0)], 1 - slot)


def _moe(rows, row_tok, blk_lo, blk_hi, nval, wg, wu, wd, g_ffn, g_fin, bm):
    n = rows.shape[0]
    nblk = row_tok.shape[0]
    w_in = lambda tab: pl.BlockSpec((1, D_MODEL, D_EXPERT), lambda i, lo, hi, nv: ((lo, hi)[tab][i], 0, 0))
    w_dn = lambda tab: pl.BlockSpec((1, D_EXPERT, D_MODEL), lambda i, lo, hi, nv: ((lo, hi)[tab][i], 0, 0))
    vec = pl.BlockSpec((1, D_MODEL), lambda i, lo, hi, nv: (0, 0))
    grid_spec = pltpu.PrefetchScalarGridSpec(
        num_scalar_prefetch=3,
        grid=(nblk,),
        in_specs=[pl.BlockSpec((1, 1, bm), lambda i, lo, hi, nv: (i, 0, 0), memory_space=pltpu.SMEM),
                  pl.BlockSpec((1, 1, bm), lambda i, lo, hi, nv: (jnp.minimum(i + 1, nblk - 1), 0, 0),
                               memory_space=pltpu.SMEM),
                  pl.BlockSpec(memory_space=pl.ANY),
                  w_in(0), w_in(0), w_dn(0), w_in(1), w_in(1), w_dn(1), vec, vec],
        out_specs=pl.BlockSpec(memory_space=pl.ANY),
        scratch_shapes=[pltpu.VMEM((2, bm, ROW_W), F32), pltpu.VMEM((2, bm, D_MODEL), F32),
                        pltpu.SemaphoreType.DMA((2,)), pltpu.SemaphoreType.DMA((2,))],
    )
    return pl.pallas_call(
        _moe_kernel,
        grid_spec=grid_spec,
        out_shape=jax.ShapeDtypeStruct((n, D_MODEL), F32),
        compiler_params=_params("arbitrary"),
        name="moe",
    )(blk_lo, blk_hi, nval, row_tok, row_tok, rows, wg, wu, wd, wg, wu, wd, g_ffn, g_fin)


def _tile(s, pref):
    t = min(pref, s)
    while s % t:
        t -= ATTN_BLK
    return t


def _encode(x, meta_tokens, p, moe_bm=128):
    b, s, _ = x.shape
    assert s % ATTN_BLK == 0
    tm = _tile(s, 512)
    q, kk, vv, xr, yg = _proj(x, p['ln_mix_g'], p['w_in'], _rope_tabs(N_META, s), tm)
    _, kkm, vvm, xrm, _ = _proj(meta_tokens[None], p['ln_mix_g'], p['w_in'], _rope_tabs(0, N_META), N_META)
    attn_n = _attn(q, kk, vv, kkm[0], vvm[0], p['attn_sink'], p['g_attn_out'])
    tr = _tile(s, 256)
    rec_args = (xr, xrm[0], p['conv_w'], p['conv_b'])
    hf = _rec(*rec_args, p['wcat_fwd'], p['b_rgate_fwd'], p['b_igate_fwd'], p['nsp_fwd'], tr, False)
    hb = _rec(*rec_args, p['wcat_bwd'], p['b_rgate_bwd'], p['b_igate_bwd'], p['nsp_bwd'], tr, True)
    rows = _outproj(attn_n, hf, hb, yg, x, p['w_out'], p['g_rec_out'], p['ln_ffn_g'],
                    p['w_router'], p['b_router'], tm)
    rows = rows.reshape(b * s, ROW_W)
    cls = rows[:, D_MODEL + 2].astype(jnp.int32)
    row_tok, blk_lo, blk_hi, nval = _route_plan(cls, moe_bm)
    out = _moe(rows, row_tok, blk_lo, blk_hi, nval, p['w_exp_gate'], p['w_exp_up'], p['w_exp_down'],
               p['ln_ffn_g'], p['ln_final_g'], moe_bm)
    return out.reshape(b, s, D_MODEL)


def _prepare(ln_mix_g, w_in, conv_w, conv_b, w_rgate_fwd, b_rgate_fwd, w_igate_fwd, b_igate_fwd, lam_fwd,
             w_rgate_bwd, b_rgate_bwd, w_igate_bwd, b_igate_bwd, lam_bwd, attn_sink, g_attn_out, g_rec_out,
             w_out, ln_ffn_g, w_router_grp, b_router_grp, w_router_exp, b_router_exp,
             w_exp_gate, w_exp_up, w_exp_down, ln_final_g):
    row = lambda a: a.reshape(1, -1).astype(F32)
    pad = LANES - N_EXPERTS - N_GROUPS
    w_router = jnp.concatenate([w_router_exp[0], w_router_grp[0], jnp.zeros((D_MODEL, pad), F32)], axis=1)
    b_router = jnp.concatenate([b_router_exp[0], b_router_grp[0], jnp.zeros((pad,), F32)])
    return {
        'ln_mix_g': row(ln_mix_g[0]), 'w_in': w_in[0].astype(BF16),
        'conv_w': conv_w[0].astype(F32), 'conv_b': row(conv_b[0]),
        'wcat_fwd': _gate_weights(w_rgate_fwd[0], w_igate_fwd[0]),
        'wcat_bwd': _gate_weights(w_rgate_bwd[0], w_igate_bwd[0]),
        'b_rgate_fwd': row(b_rgate_fwd[0]), 'b_igate_fwd': row(b_igate_fwd[0]),
        'b_rgate_bwd': row(b_rgate_bwd[0]), 'b_igate_bwd': row(b_igate_bwd[0]),
        'nsp_fwd': row(-RG_C * jax.nn.softplus(-lam_fwd[0])), 'nsp_bwd': row(-RG_C * jax.nn.softplus(-lam_bwd[0])),
        'attn_sink': row(attn_sink[0]), 'g_attn_out': row(g_attn_out[0]), 'g_rec_out': row(g_rec_out[0]),
        'w_out': w_out[0].astype(BF16), 'ln_ffn_g': row(ln_ffn_g[0]),
        'w_router': w_router.astype(BF16), 'b_router': row(b_router),
        'w_exp_gate': w_exp_gate[0].astype(BF16), 'w_exp_up': w_exp_up[0].astype(BF16),
        'w_exp_down': w_exp_down[0].astype(BF16), 'ln_final_g': row(ln_final_g),
    }


def kernel(x_prompt, x_sample, meta_tokens, ln_mix_g, w_in, conv_w, conv_b, w_rgate_fwd, b_rgate_fwd, w_igate_fwd, b_igate_fwd, lam_fwd, w_rgate_bwd, b_rgate_bwd, w_igate_bwd, b_igate_bwd, lam_bwd, attn_sink, g_attn_out, g_rec_out, w_out, ln_ffn_g, w_router_grp, b_router_grp, w_router_exp, b_router_exp, w_exp_gate, w_exp_up, w_exp_down, ln_final_g):
    p = _prepare(ln_mix_g, w_in, conv_w, conv_b, w_rgate_fwd, b_rgate_fwd, w_igate_fwd, b_igate_fwd, lam_fwd,
                 w_rgate_bwd, b_rgate_bwd, w_igate_bwd, b_igate_bwd, lam_bwd, attn_sink, g_attn_out, g_rec_out,
                 w_out, ln_ffn_g, w_router_grp, b_router_grp, w_router_exp, b_router_exp,
                 w_exp_gate, w_exp_up, w_exp_down, ln_final_g)
    return (_encode(x_prompt, meta_tokens, p), _encode(x_sample, meta_tokens, p))
```

```python
import functools

import numpy as np
import jax
import jax.numpy as jnp
from jax import lax
from jax.experimental import pallas as pl
from jax.experimental.pallas import tpu as pltpu

D_MODEL = 1024
N_META = 16
HEAD_DIM = 64
N_HEADS = 8
N_KV_HEADS = 2
Q_PER_KV = N_HEADS // N_KV_HEADS
ATTN_W = N_HEADS * HEAD_DIM
KV_W = N_KV_HEADS * HEAD_DIM
REC_W = D_MODEL // 2
REC_BLOCKS = 8
REC_BW = REC_W // REC_BLOCKS
IN_COLS = ATTN_W + 2 * KV_W + 2 * REC_W
CONV_W = 4
CONV_LEFT = 2
RG_C = 8.0
WINDOW = 128
ATTN_BLK = 128
ROT_DIM = HEAD_DIM // 4
ROPE_THETA = 500000.0
N_GROUPS = 4
EXP_PER_GROUP = 8
N_EXPERTS = N_GROUPS * EXP_PER_GROUP
D_EXPERT = D_MODEL // 2
EPS = 1e-6

LANES = 128
SUBLANES = 8
PAIRS_PER_GROUP = EXP_PER_GROUP * (EXP_PER_GROUP - 1) // 2
N_CLASSES = N_GROUPS * PAIRS_PER_GROUP
ROW_W = D_MODEL + LANES
NEG_BIG = -1e30
VMEM_LIMIT = 48 * 1024 * 1024

F32 = jnp.float32
BF16 = jnp.bfloat16


def _rms(x, g):
    ms = jnp.mean(x * x, axis=-1, keepdims=True)
    return x * lax.rsqrt(ms + EPS) * g


def _sigmoid(x):
    return 0.5 * jnp.tanh(0.5 * x) + 0.5


def _params(*sem):
    return pltpu.CompilerParams(dimension_semantics=sem, vmem_limit_bytes=VMEM_LIMIT)


def _proj_kernel(x_ref, g_ref, w_ref, c_ref, sa_ref, sb_ref, q_ref, kk_ref, vv_ref, xr_ref, yg_ref):
    xn = _rms(x_ref[0], g_ref[...])
    p = jnp.dot(xn.astype(BF16), w_ref[...], preferred_element_type=F32)
    c, sa, sb = c_ref[...], sa_ref[...], sb_ref[...]

    def rope(t):
        return t * c + pltpu.roll(t, LANES - ROT_DIM // 2, 1) * sa + pltpu.roll(t, ROT_DIM // 2, 1) * sb

    for j in range(ATTN_W // LANES):
        qj = rope(p[:, j * LANES:(j + 1) * LANES]) * (HEAD_DIM ** -0.5)
        q_ref[0, :, j * LANES:(j + 1) * LANES] = qj.astype(BF16)
    k = rope(p[:, ATTN_W:ATTN_W + KV_W])
    v = p[:, ATTN_W + KV_W:ATTN_W + 2 * KV_W]
    kk_ref[0, :, :KV_W] = k.astype(BF16)
    kk_ref[0, :, KV_W:] = pltpu.roll(k, HEAD_DIM, 1).astype(BF16)
    vv_ref[0, :, :KV_W] = v.astype(BF16)
    vv_ref[0, :, KV_W:] = pltpu.roll(v, HEAD_DIM, 1).astype(BF16)
    o = ATTN_W + 2 * KV_W
    xr_ref[0] = p[:, o:o + REC_W]
    yg_ref[0] = p[:, o + REC_W:o + 2 * REC_W].astype(BF16)


def _proj(x, g, w_in, tabs, tm):
    b, s, _ = x.shape
    tab_spec = pl.BlockSpec((tm, LANES), lambda i, j: (j, 0))
    row = lambda w: pl.BlockSpec((1, tm, w), lambda i, j: (i, j, 0))
    return pl.pallas_call(
        _proj_kernel,
        grid=(b, s // tm),
        in_specs=[row(D_MODEL),
                  pl.BlockSpec((1, D_MODEL), lambda i, j: (0, 0)),
                  pl.BlockSpec((D_MODEL, IN_COLS), lambda i, j: (0, 0)),
                  tab_spec, tab_spec, tab_spec],
        out_specs=[row(ATTN_W), row(2 * KV_W), row(2 * KV_W), row(REC_W), row(REC_W)],
        out_shape=[jax.ShapeDtypeStruct((b, s, ATTN_W), BF16),
                   jax.ShapeDtypeStruct((b, s, 2 * KV_W), BF16),
                   jax.ShapeDtypeStruct((b, s, 2 * KV_W), BF16),
                   jax.ShapeDtypeStruct((b, s, REC_W), F32),
                   jax.ShapeDtypeStruct((b, s, REC_W), BF16)],
        compiler_params=_params("parallel", "parallel"),
        name="proj",
    )(x, g, w_in, *tabs)


def _rope_tabs(pos0, n):
    half = ROT_DIM // 2
    inv = ROPE_THETA ** (-jnp.arange(0, ROT_DIM, 2, dtype=F32) / ROT_DIM)
    ang = jnp.arange(pos0 + n, dtype=F32)[pos0:, None] * inv[None, :]
    cos, sin = jnp.cos(ang), jnp.sin(ang)
    one = jnp.ones((n, HEAD_DIM - ROT_DIM), F32)
    zero = jnp.zeros((n, HEAD_DIM - ROT_DIM), F32)
    zh = jnp.zeros((n, half), F32)
    c = jnp.concatenate([cos, cos, one], axis=1)
    sa = jnp.concatenate([-sin, zh, zero], axis=1)
    sb = jnp.concatenate([zh, sin, zero], axis=1)
    return tuple(jnp.concatenate([t, t], axis=1) for t in (c, sa, sb))


def _attn_kernel(q_ref, kp_ref, kc_ref, kn_ref, km_ref, vp_ref, vc_ref, vn_ref, vm_ref,
                 sink_ref, g_ref, o_ref):
    n = pl.program_id(1)
    nb = pl.num_programs(1)
    blk = ATTN_BLK
    kall = jnp.concatenate([kp_ref[0], kc_ref[0], kn_ref[0], km_ref[...]], axis=0)
    vall = jnp.concatenate([vp_ref[0], vc_ref[0], vn_ref[0], vm_ref[...]], axis=0)
    nk = kall.shape[0]
    r = lax.broadcasted_iota(jnp.int32, (blk, nk), 0)
    c = lax.broadcasted_iota(jnp.int32, (blk, nk), 1)
    off_prev = jnp.where(n > 0, 0, blk)
    off_next = jnp.where(n < nb - 1, 0, blk)
    valid = (((c < blk) & (c >= r + off_prev)) | ((c >= blk) & (c < 2 * blk))
             | ((c >= 2 * blk) & (c < 3 * blk) & (c - 2 * blk <= r - off_next)) | (c >= 3 * blk))
    lane = lax.broadcasted_iota(jnp.int32, (blk, LANES), 1)
    low = lane < HEAD_DIM
    cols = []
    for j in range(ATTN_W // LANES):
        qj = q_ref[0, :, j * LANES:(j + 1) * LANES]
        g = (2 * j) // Q_PER_KV
        outs = []
        for half in range(2):
            h = 2 * j + half
            qh = jnp.where(low if half == 0 else ~low, qj, jnp.zeros_like(qj))
            sel = slice(0, KV_W) if half == g else slice(KV_W, 2 * KV_W)
            s = lax.dot_general(qh, kall[:, sel], (((1,), (1,)), ((), ())), preferred_element_type=F32)
            s = jnp.where(valid, s, NEG_BIG)
            sk = sink_ref[:, h:h + 1]
            m = jnp.maximum(jnp.max(s, axis=-1, keepdims=True), sk)
            p = jnp.exp(s - m)
            den = jnp.sum(p, axis=-1, keepdims=True) + jnp.exp(sk - m)
            pv = jnp.dot(p.astype(BF16), vall[:, sel], preferred_element_type=F32)
            outs.append(pv / den)
        cols.append(jnp.where(low, outs[0], outs[1]))
    attn = jnp.concatenate(cols, axis=1)
    o_ref[0] = _rms(attn, g_ref[...]).astype(BF16)


def _attn(q, kk, vv, kkm, vvm, sink, g):
    b, s, _ = q.shape
    nb = s // ATTN_BLK
    prev = lambda i, n: (i, jnp.maximum(n - 1, 0), 0)
    cur = lambda i, n: (i, n, 0)
    nxt = lambda i, n: (i, jnp.minimum(n + 1, nb - 1), 0)
    kvs = lambda f: pl.BlockSpec((1, ATTN_BLK, 2 * KV_W), f)
    meta = pl.BlockSpec((N_META, 2 * KV_W), lambda i, n: (0, 0))
    return pl.pallas_call(
        _attn_kernel,
        grid=(b, nb),
        in_specs=[pl.BlockSpec((1, ATTN_BLK, ATTN_W), cur),
                  kvs(prev), kvs(cur), kvs(nxt), meta,
                  kvs(prev), kvs(cur), kvs(nxt), meta,
                  pl.BlockSpec((1, N_HEADS), lambda i, n: (0, 0)),
                  pl.BlockSpec((1, ATTN_W), lambda i, n: (0, 0))],
        out_specs=pl.BlockSpec((1, ATTN_BLK, ATTN_W), cur),
        out_shape=jax.ShapeDtypeStruct((b, s, ATTN_W), BF16),
        compiler_params=_params("parallel", "parallel"),
        name="attn",
    )(q, kk, kk, kk, kkm, vv, vv, vv, vvm, sink, g)


def _scan_rows(a, u, carry, reverse):
    rows, width = a.shape
    row = lax.broadcasted_iota(jnp.int32, (SUBLANES, width), 0)
    out = [None] * (rows // SUBLANES)
    order = range(rows // SUBLANES)
    for sidx in (reversed(order) if reverse else order):
        a8 = a[sidx * SUBLANES:(sidx + 1) * SUBLANES]
        u8 = u[sidx * SUBLANES:(sidx + 1) * SUBLANES]
        d = 1
        while d < SUBLANES:
            shift = SUBLANES - d if reverse else d
            keep = (row < SUBLANES - d) if reverse else (row >= d)
            a_s = pltpu.roll(a8, shift, 0)
            u_s = pltpu.roll(u8, shift, 0)
            u8 = jnp.where(keep, a8 * u_s + u8, u8)
            a8 = jnp.where(keep, a8 * a_s, a8)
            d *= 2
        h8 = a8 * carry + u8
        carry = h8[0:1] if reverse else h8[SUBLANES - 1:SUBLANES]
        out[sidx] = h8
    return jnp.concatenate(out, axis=0), carry


def _gate_scan(xc, w_ref, br, bi, nsp, carry, reverse):
    hs, cs = [], []
    half = REC_W // 2
    for ch in range(2):
        sl = slice(ch * half, (ch + 1) * half)
        xcc = xc[:, sl]
        pre = jnp.dot(xcc.astype(BF16), w_ref[ch], preferred_element_type=F32)
        gate_r = _sigmoid(pre[:, :half] + br[:, sl])
        gate_i = _sigmoid(pre[:, half:] + bi[:, sl])
        log_a = nsp[:, sl] * gate_r
        a = jnp.exp(log_a)
        mult = jnp.sqrt(-jnp.tanh(log_a) * (1.0 + a * a))
        u = mult * (gate_i * xcc)
        h, c = _scan_rows(a, u, carry[:, sl], reverse)
        hs.append(h)
        cs.append(c)
    return jnp.concatenate(hs, axis=1), jnp.concatenate(cs, axis=1)


def _rec_kernel(xp_ref, x_ref, xn_ref, xm_ref, cw_ref, cb_ref, w_ref, br_ref, bi_ref, nsp_ref,
                h_ref, xs_ref, carry_ref, *, tm, reverse):
    j = pl.program_id(1)
    nt = pl.num_programs(1)
    first_tile = (j == nt - 1) if reverse else (j == 0)
    last_tile = (j == 0) if reverse else (j == nt - 1)
    cw = cw_ref[...]
    cb = cb_ref[...]
    br, bi, nsp = br_ref[...], bi_ref[...], nsp_ref[...]
    halo = SUBLANES

    def conv(rows):
        acc = cb + cw[0:1] * xs_ref[halo - CONV_LEFT:halo - CONV_LEFT + rows]
        for t in range(1, CONV_W):
            acc = acc + cw[t:t + 1] * xs_ref[halo - CONV_LEFT + t:halo - CONV_LEFT + t + rows]
        return acc

    if reverse:
        @pl.when(j == 0)
        def _():
            carry_ref[...] = jnp.zeros_like(carry_ref)
    else:
        @pl.when(j == 0)
        def _():
            xs_ref[0:halo] = jnp.zeros((halo, REC_W), F32)
            xs_ref[halo:halo + N_META] = xm_ref[...]
            xs_ref[halo + N_META:2 * halo + N_META] = x_ref[0, 0:halo]
            _, c = _gate_scan(conv(N_META), w_ref, br, bi, nsp, jnp.zeros((1, REC_W), F32), False)
            carry_ref[0:1] = c

    xs_ref[0:halo] = jnp.where(first_tile, xm_ref[N_META - halo:N_META], xp_ref[0])
    xs_ref[halo:halo + tm] = x_ref[0]
    xs_ref[halo + tm:2 * halo + tm] = jnp.where(last_tile, jnp.zeros((halo, REC_W), F32), xn_ref[0])
    h, c = _gate_scan(conv(tm), w_ref, br, bi, nsp, carry_ref[0:1], reverse)
    h_ref[0] = h
    carry_ref[0:1] = c


def _rec(xr, xr_meta, cw, cb, wcat, br, bi, nsp, tm, reverse):
    b, s, _ = xr.shape
    nt = s // tm
    per = tm // SUBLANES
    nh = s // SUBLANES
    t_of = (lambda j: nt - 1 - j) if reverse else (lambda j: j)
    vec = lambda r: pl.BlockSpec((r, REC_W), lambda i, j: (0, 0))
    return pl.pallas_call(
        functools.partial(_rec_kernel, tm=tm, reverse=reverse),
        grid=(b, nt),
        in_specs=[pl.BlockSpec((1, SUBLANES, REC_W), lambda i, j: (i, jnp.maximum(t_of(j) * per - 1, 0), 0)),
                  pl.BlockSpec((1, tm, REC_W), lambda i, j: (i, t_of(j), 0)),
                  pl.BlockSpec((1, SUBLANES, REC_W),
                               lambda i, j: (i, jnp.minimum((t_of(j) + 1) * per, nh - 1), 0)),
                  vec(N_META), vec(CONV_W), vec(1),
                  pl.BlockSpec((2, REC_W // 2, REC_W), lambda i, j: (0, 0, 0)),
                  vec(1), vec(1), vec(1)],
        out_specs=pl.BlockSpec((1, tm, REC_W), lambda i, j: (i, t_of(j), 0)),
        out_shape=jax.ShapeDtypeStruct((b, s, REC_W), F32),
        scratch_shapes=[pltpu.VMEM((tm + 2 * SUBLANES, REC_W), F32), pltpu.VMEM((SUBLANES, REC_W), F32)],
        compiler_params=_params("parallel", "arbitrary"),
        name="rec_bwd" if reverse else "rec_fwd",
    )(xr, xr, xr, xr_meta, cw, cb, wcat, br, bi, nsp)


def _gate_weights(w_r, w_i):
    per = REC_BLOCKS // 2

    def bd(w4):
        z = jnp.zeros((per * REC_BW, per * REC_BW), F32)
        for n in range(per):
            z = z.at[n * REC_BW:(n + 1) * REC_BW, n * REC_BW:(n + 1) * REC_BW].set(w4[n])
        return z

    halves = [jnp.concatenate([bd(w_r[c * per:(c + 1) * per]), bd(w_i[c * per:(c + 1) * per])], axis=1)
              for c in range(2)]
    return jnp.stack(halves).astype(BF16)


def _pair_index(a, b):
    return (a * (2 * EXP_PER_GROUP - 1 - a)) // 2 + (b - a - 1)


def _outproj_kernel(at_ref, hf_ref, hb_ref, yg_ref, x_ref, w_ref, grec_ref, gffn_ref, wr_ref, brt_ref, o_ref):
    yg = yg_ref[0].astype(F32)
    gelu = 0.5 * yg * (1.0 + jnp.tanh(np.sqrt(2.0 / np.pi).astype(np.float32) * (yg + 0.044715 * (yg * yg * yg))))
    rec = (hf_ref[0] + hb_ref[0]) * gelu
    mixed = jnp.concatenate([at_ref[0], _rms(rec, grec_ref[...]).astype(BF16)], axis=1)
    h1 = x_ref[0] + jnp.dot(mixed, w_ref[...], preferred_element_type=F32)
    o_ref[0, :, :D_MODEL] = h1

    u = _rms(h1, gffn_ref[...]).astype(BF16)
    lg = jnp.dot(u, wr_ref[...], preferred_element_type=F32) + brt_ref[...]
    lane_i = lax.broadcasted_iota(jnp.int32, lg.shape, 1)
    lane = lane_i.astype(F32)
    lane_grp = (lane_i >> 3).astype(F32)
    ninf = -jnp.inf
    big = jnp.float32(1 << 20)
    is_grp = (lane_i >= N_EXPERTS) & (lane_i < N_EXPERTS + N_GROUPS)
    gl = jnp.where(is_grp, lg, ninf)
    gmax = jnp.max(gl, axis=-1, keepdims=True)
    gidx = jnp.min(jnp.where(gl == gmax, lane - N_EXPERTS, big), axis=-1, keepdims=True)
    p_grp = 1.0 / jnp.sum(jnp.where(is_grp, jnp.exp(gl - gmax), 0.0), axis=-1, keepdims=True)
    in_grp = (lane_i < N_EXPERTS) & (lane_grp == gidx)
    el = jnp.where(in_grp, lg, ninf)
    m1 = jnp.max(el, axis=-1, keepdims=True)
    i1 = jnp.min(jnp.where(el == m1, lane, big), axis=-1, keepdims=True)
    el2 = jnp.where(lane == i1, ninf, el)
    m2 = jnp.max(el2, axis=-1, keepdims=True)
    i2 = jnp.min(jnp.where(el2 == m2, lane, big), axis=-1, keepdims=True)
    e2 = jnp.exp(m2 - m1)
    g1 = p_grp / (1.0 + e2)
    g2 = p_grp * e2 / (1.0 + e2)
    first_lo = i1 < i2
    lo = jnp.where(first_lo, i1, i2) - gidx * EXP_PER_GROUP
    hi = jnp.where(first_lo, i2, i1) - gidx * EXP_PER_GROUP
    cls = gidx * PAIRS_PER_GROUP + 0.5 * (lo * (2 * EXP_PER_GROUP - 1 - lo)) + (hi - lo - 1.0)
    g_lo = jnp.where(first_lo, g1, g2)
    g_hi = jnp.where(first_lo, g2, g1)
    tl = lax.broadcasted_iota(jnp.int32, (lg.shape[0], LANES), 1)
    o_ref[0, :, D_MODEL:] = jnp.where(tl == 0, g_lo, jnp.where(tl == 1, g_hi, jnp.where(tl == 2, cls, 0.0)))


def _outproj(attn_n, hf, hb, yg, x, w_out, g_rec, g_ffn, w_router, b_router, tm):
    b, s, _ = x.shape
    row = lambda w: pl.BlockSpec((1, tm, w), lambda i, j: (i, j, 0))
    full = lambda r, c: pl.BlockSpec((r, c), lambda i, j: (0, 0))
    return pl.pallas_call(
        _outproj_kernel,
        grid=(b, s // tm),
        in_specs=[row(ATTN_W), row(REC_W), row(REC_W), row(REC_W), row(D_MODEL),
                  full(D_MODEL, D_MODEL), full(1, REC_W), full(1, D_MODEL),
                  full(D_MODEL, LANES), full(1, LANES)],
        out_specs=row(ROW_W),
        out_shape=jax.ShapeDtypeStruct((b, s, ROW_W), F32),
        compiler_params=_params("parallel", "parallel"),
        name="outproj",
    )(attn_n, hf, hb, yg, x, w_out, g_rec, g_ffn, w_router, b_router)


def _class_tables():
    lo = np.zeros((N_CLASSES,), np.int32)
    hi = np.zeros((N_CLASSES,), np.int32)
    for g in range(N_GROUPS):
        for a in range(EXP_PER_GROUP):
            for b in range(a + 1, EXP_PER_GROUP):
                c = g * PAIRS_PER_GROUP + _pair_index(a, b)
                lo[c] = g * EXP_PER_GROUP + a
                hi[c] = g * EXP_PER_GROUP + b
    return lo, hi


def _route_plan(cls, bm):
    n = cls.shape[0]
    nblk = n // bm + N_CLASSES
    i32 = jnp.int32
    order = jnp.argsort(cls, stable=True).astype(i32)
    counts = jnp.sum((cls[:, None] == jnp.arange(N_CLASSES, dtype=i32)[None, :]).astype(i32), axis=0)
    start = jnp.cumsum(counts) - counts
    padded = (counts + bm - 1) // bm * bm
    pend = jnp.cumsum(padded)
    pstart = pend - padded
    brow = jnp.arange(nblk, dtype=i32) * bm
    blk_cls = jnp.minimum(jnp.searchsorted(pend, brow, side='right').astype(i32), N_CLASSES - 1)
    nval = jnp.clip(counts[blk_cls] - (brow - pstart[blk_cls]), 0, bm)
    nval = jnp.where(brow < pend[-1], nval, 0).astype(i32)
    idx = jnp.arange(bm, dtype=i32)[None, :] + (brow - pstart[blk_cls] + start[blk_cls])[:, None]
    tok = order[jnp.clip(idx, 0, n - 1)]
    row_tok = jnp.where(jnp.arange(bm, dtype=i32)[None, :] < nval[:, None], tok, 0).astype(i32)
    lo_t, hi_t = _class_tables()
    return row_tok.reshape(nblk, 1, bm), jnp.asarray(lo_t)[blk_cls], jnp.asarray(hi_t)[blk_cls], nval


def _moe_kernel(lo_ref, hi_ref, nval_ref, rt_ref, rtn_ref, h_hbm,
                wg_lo, wu_lo, wd_lo, wg_hi, wu_hi, wd_hi, gffn_ref, gfin_ref,
                out_hbm, xbuf, ybuf, gsem, ssem):
    b = pl.program_id(0)
    nb = pl.num_programs(0)
    slot = b & 1
    nv = nval_ref[b]

    def gather_copy(tok, r, s):
        return pltpu.make_async_copy(h_hbm.at[pl.ds(tok, 1)], xbuf.at[s, pl.ds(r, 1)], gsem.at[s])

    def scatter_copy(tok, r, s):
        return pltpu.make_async_copy(ybuf.at[s, pl.ds(r, 1)], out_hbm.at[pl.ds(tok, 1)], ssem.at[s])

    bm = xbuf.shape[1]

    def for_rows(count, fn):
        @pl.when(count == bm)
        def _():
            for r in range(bm):
                fn(r)

        @pl.when(count < bm)
        def _():
            def body(r, carry):
                fn(r)
                return carry
            lax.fori_loop(0, count, body, 0)

    def gather_start(rt, count, s):
        for_rows(count, lambda r: gather_copy(rt[0, 0, r], r, s).start())

    def gather_wait(count, s):
        for_rows(count, lambda r: gather_copy(0, 0, s).wait())

    def scatter_start(rt, count, s):
        for_rows(count, lambda r: scatter_copy(rt[0, 0, r], r, s).start())

    def scatter_wait(count, s):
        for_rows(count, lambda r: scatter_copy(0, 0, s).wait())

    @pl.when(b == 0)
    def _():
        xbuf[...] = jnp.zeros_like(xbuf)
        gather_start(rt_ref, nv, 0)

    gather_wait(nv, slot)

    @pl.when(b + 1 < nb)
    def _():
        gather_start(rtn_ref, nval_ref[jnp.minimum(b + 1, nb - 1)], 1 - slot)

    @pl.when(b >= 2)
    def _():
        scatter_wait(nval_ref[jnp.maximum(b - 2, 0)], slot)

    @pl.when(nv > 0)
    def _():
        x = xbuf[slot]
        h1 = x[:, :D_MODEL]
        g_lo = x[:, D_MODEL:D_MODEL + 1]
        g_hi = x[:, D_MODEL + 1:D_MODEL + 2]
        u = _rms(h1, gffn_ref[...]).astype(BF16)

        def expert(wg, wu, wd):
            a = jnp.dot(u, wg[0], preferred_element_type=F32)
            c = jnp.dot(u, wu[0], preferred_element_type=F32)
            hid = (a * _sigmoid(a)) * c
            return jnp.dot(hid.astype(BF16), wd[0], preferred_element_type=F32)

        y = g_lo * expert(wg_lo, wu_lo, wd_lo) + g_hi * expert(wg_hi, wu_hi, wd_hi)
        ybuf[slot] = _rms(h1 + y, gfin_ref[...])

    scatter_start(rt_ref, nv, slot)

    @pl.when(b == nb - 1)
    def _():
        scatter_wait(nv, slot)

        @pl.when(b >= 1)
        def _():
            scatter_wait(nval_ref[jnp.maximum(b - 1, 0)], 1 - slot)


def _moe(rows, row_tok, blk_lo, blk_hi, nval, wg, wu, wd, g_ffn, g_fin, bm):
    n = rows.shape[0]
    nblk = row_tok.shape[0]
    w_in = lambda tab: pl.BlockSpec((1, D_MODEL, D_EXPERT), lambda i, lo, hi, nv: ((lo, hi)[tab][i], 0, 0))
    w_dn = lambda tab: pl.BlockSpec((1, D_EXPERT, D_MODEL), lambda i, lo, hi, nv: ((lo, hi)[tab][i], 0, 0))
    vec = pl.BlockSpec((1, D_MODEL), lambda i, lo, hi, nv: (0, 0))
    grid_spec = pltpu.PrefetchScalarGridSpec(
        num_scalar_prefetch=3,
        grid=(nblk,),
        in_specs=[pl.BlockSpec((1, 1, bm), lambda i, lo, hi, nv: (i, 0, 0), memory_space=pltpu.SMEM),
                  pl.BlockSpec((1, 1, bm), lambda i, lo, hi, nv: (jnp.minimum(i + 1, nblk - 1), 0, 0),
                               memory_space=pltpu.SMEM),
                  pl.BlockSpec(memory_space=pl.ANY),
                  w_in(0), w_in(0), w_dn(0), w_in(1), w_in(1), w_dn(1), vec, vec],
        out_specs=pl.BlockSpec(memory_space=pl.ANY),
        scratch_shapes=[pltpu.VMEM((2, bm, ROW_W), F32), pltpu.VMEM((2, bm, D_MODEL), F32),
                        pltpu.SemaphoreType.DMA((2,)), pltpu.SemaphoreType.DMA((2,))],
    )
    return pl.pallas_call(
        _moe_kernel,
        grid_spec=grid_spec,
        out_shape=jax.ShapeDtypeStruct((n, D_MODEL), F32),
        compiler_params=_params("arbitrary"),
        name="moe",
    )(blk_lo, blk_hi, nval, row_tok, row_tok, rows, wg, wu, wd, wg, wu, wd, g_ffn, g_fin)


def _tile(s, pref):
    t = min(pref, s)
    while s % t:
        t -= ATTN_BLK
    return t


def _encode(x, meta_tokens, p, moe_bm=128):
    b, s, _ = x.shape
    assert s % ATTN_BLK == 0
    tm = _tile(s, 512)
    q, kk, vv, xr, yg = _proj(x, p['ln_mix_g'], p['w_in'], _rope_tabs(N_META, s), tm)
    _, kkm, vvm, xrm, _ = _proj(meta_tokens[None], p['ln_mix_g'], p['w_in'], _rope_tabs(0, N_META), N_META)
    attn_n = _attn(q, kk, vv, kkm[0], vvm[0], p['attn_sink'], p['g_attn_out'])
    tr = _tile(s, 256)
    rec_args = (xr, xrm[0], p['conv_w'], p['conv_b'])
    hf = _rec(*rec_args, p['wcat_fwd'], p['b_rgate_fwd'], p['b_igate_fwd'], p['nsp_fwd'], tr, False)
    hb = _rec(*rec_args, p['wcat_bwd'], p['b_rgate_bwd'], p['b_igate_bwd'], p['nsp_bwd'], tr, True)
    rows = _outproj(attn_n, hf, hb, yg, x, p['w_out'], p['g_rec_out'], p['ln_ffn_g'],
                    p['w_router'], p['b_router'], tm)
    rows = rows.reshape(b * s, ROW_W)
    cls = rows[:, D_MODEL + 2].astype(jnp.int32)
    row_tok, blk_lo, blk_hi, nval = _route_plan(cls, moe_bm)
    out = _moe(rows, row_tok, blk_lo, blk_hi, nval, p['w_exp_gate'], p['w_exp_up'], p['w_exp_down'],
               p['ln_ffn_g'], p['ln_final_g'], moe_bm)
    return out.reshape(b, s, D_MODEL)


def _prepare(ln_mix_g, w_in, conv_w, conv_b, w_rgate_fwd, b_rgate_fwd, w_igate_fwd, b_igate_fwd, lam_fwd,
             w_rgate_bwd, b_rgate_bwd, w_igate_bwd, b_igate_bwd, lam_bwd, attn_sink, g_attn_out, g_rec_out,
             w_out, ln_ffn_g, w_router_grp, b_router_grp, w_router_exp, b_router_exp,
             w_exp_gate, w_exp_up, w_exp_down, ln_final_g):
    row = lambda a: a.reshape(1, -1).astype(F32)
    pad = LANES - N_EXPERTS - N_GROUPS
    w_router = jnp.concatenate([w_router_exp[0], w_router_grp[0], jnp.zeros((D_MODEL, pad), F32)], axis=1)
    b_router = jnp.concatenate([b_router_exp[0], b_router_grp[0], jnp.zeros((pad,), F32)])
    return {
        'ln_mix_g': row(ln_mix_g[0]), 'w_in': w_in[0].astype(BF16),
        'conv_w': conv_w[0].astype(F32), 'conv_b': row(conv_b[0]),
        'wcat_fwd': _gate_weights(w_rgate_fwd[0], w_igate_fwd[0]),
        'wcat_bwd': _gate_weights(w_rgate_bwd[0], w_igate_bwd[0]),
        'b_rgate_fwd': row(b_rgate_fwd[0]), 'b_igate_fwd': row(b_igate_fwd[0]),
        'b_rgate_bwd': row(b_rgate_bwd[0]), 'b_igate_bwd': row(b_igate_bwd[0]),
        'nsp_fwd': row(-RG_C * jax.nn.softplus(-lam_fwd[0])), 'nsp_bwd': row(-RG_C * jax.nn.softplus(-lam_bwd[0])),
        'attn_sink': row(attn_sink[0]), 'g_attn_out': row(g_attn_out[0]), 'g_rec_out': row(g_rec_out[0]),
        'w_out': w_out[0].astype(BF16), 'ln_ffn_g': row(ln_ffn_g[0]),
        'w_router': w_router.astype(BF16), 'b_router': row(b_router),
        'w_exp_gate': w_exp_gate[0].astype(BF16), 'w_exp_up': w_exp_up[0].astype(BF16),
        'w_exp_down': w_exp_down[0].astype(BF16), 'ln_final_g': row(ln_final_g),
    }


def kernel(x_prompt, x_sample, meta_tokens, ln_mix_g, w_in, conv_w, conv_b, w_rgate_fwd, b_rgate_fwd, w_igate_fwd, b_igate_fwd, lam_fwd, w_rgate_bwd, b_rgate_bwd, w_igate_bwd, b_igate_bwd, lam_bwd, attn_sink, g_attn_out, g_rec_out, w_out, ln_ffn_g, w_router_grp, b_router_grp, w_router_exp, b_router_exp, w_exp_gate, w_exp_up, w_exp_down, ln_final_g):
    p = _prepare(ln_mix_g, w_in, conv_w, conv_b, w_rgate_fwd, b_rgate_fwd, w_igate_fwd, b_igate_fwd, lam_fwd,
                 w_rgate_bwd, b_rgate_bwd, w_igate_bwd, b_igate_bwd, lam_bwd, attn_sink, g_attn_out, g_rec_out,
                 w_out, ln_ffn_g, w_router_grp, b_router_grp, w_router_exp, b_router_exp,
                 w_exp_gate, w_exp_up, w_exp_down, ln_final_g)
    return (_encode(x_prompt, meta_tokens, p), _encode(x_sample, meta_tokens, p))
```

```python
import functools

import numpy as np
import jax
import jax.numpy as jnp
from jax import lax
from jax.experimental import pallas as pl
from jax.experimental.pallas import tpu as pltpu

D_MODEL = 1024
N_META = 16
HEAD_DIM = 64
N_HEADS = 8
N_KV_HEADS = 2
Q_PER_KV = N_HEADS // N_KV_HEADS
ATTN_W = N_HEADS * HEAD_DIM
KV_W = N_KV_HEADS * HEAD_DIM
REC_W = D_MODEL // 2
REC_BLOCKS = 8
REC_BW = REC_W // REC_BLOCKS
IN_COLS = ATTN_W + 2 * KV_W + 2 * REC_W
CONV_W = 4
CONV_LEFT = 2
RG_C = 8.0
WINDOW = 128
ATTN_BLK = 128
ROT_DIM = HEAD_DIM // 4
ROPE_THETA = 500000.0
N_GROUPS = 4
EXP_PER_GROUP = 8
N_EXPERTS = N_GROUPS * EXP_PER_GROUP
D_EXPERT = D_MODEL // 2
EPS = 1e-6

LANES = 128
SUBLANES = 8
PAIRS_PER_GROUP = EXP_PER_GROUP * (EXP_PER_GROUP - 1) // 2
N_CLASSES = N_GROUPS * PAIRS_PER_GROUP
ROW_W = D_MODEL + LANES
NEG_BIG = -1e30
VMEM_LIMIT = 48 * 1024 * 1024

F32 = jnp.float32
BF16 = jnp.bfloat16


def _rms(x, g):
    ms = jnp.mean(x * x, axis=-1, keepdims=True)
    return x * lax.rsqrt(ms + EPS) * g


def _sigmoid(x):
    return 0.5 * jnp.tanh(0.5 * x) + 0.5


def _params(*sem):
    return pltpu.CompilerParams(dimension_semantics=sem, vmem_limit_bytes=VMEM_LIMIT)


def _proj_kernel(x_ref, g_ref, w_ref, c_ref, sa_ref, sb_ref, *out_refs, meta):
    xn = _rms(x_ref[0], g_ref[...])
    p = jnp.dot(xn.astype(BF16), w_ref[...], preferred_element_type=F32)
    c, sa, sb = c_ref[...], sa_ref[...], sb_ref[...]

    def rope(t):
        return t * c + pltpu.roll(t, LANES - ROT_DIM // 2, 1) * sa + pltpu.roll(t, ROT_DIM // 2, 1) * sb

    k = rope(p[:, ATTN_W:ATTN_W + KV_W])
    v = p[:, ATTN_W + KV_W:ATTN_W + 2 * KV_W]
    o = ATTN_W + 2 * KV_W
    if meta:
        k_ref, v_ref, xr_ref = out_refs
        k_ref[0] = k.astype(BF16)
        v_ref[0] = v.astype(BF16)
        xr_ref[0] = p[:, o:o + REC_W]
        return
    qt_ref, k_ref, vt_ref, xr_ref, yg_ref = out_refs
    q = jnp.concatenate([rope(p[:, j * LANES:(j + 1) * LANES]) * (HEAD_DIM ** -0.5)
                         for j in range(ATTN_W // LANES)], axis=1)
    for i in range(q.shape[0] // ATTN_BLK):
        qt_ref[0, i] = q[i * ATTN_BLK:(i + 1) * ATTN_BLK].T.astype(BF16)
    k_ref[0] = k.astype(BF16)
    vt_ref[0] = v.T.astype(BF16)
    xr_ref[0] = p[:, o:o + REC_W]
    yg_ref[0] = p[:, o + REC_W:o + 2 * REC_W].astype(BF16)


def _proj(x, g, w_in, tabs, tm, meta=False):
    b, s, _ = x.shape
    tab_spec = pl.BlockSpec((tm, LANES), lambda i, j: (j, 0))
    row = lambda w: pl.BlockSpec((1, tm, w), lambda i, j: (i, j, 0))
    if meta:
        out_specs = [row(KV_W), row(KV_W), row(REC_W)]
        out_shape = [jax.ShapeDtypeStruct((b, s, KV_W), BF16), jax.ShapeDtypeStruct((b, s, KV_W), BF16),
                     jax.ShapeDtypeStruct((b, s, REC_W), F32)]
    else:
        per = tm // ATTN_BLK
        out_specs = [pl.BlockSpec((1, per, ATTN_W, ATTN_BLK), lambda i, j: (i, j, 0, 0)), row(KV_W),
                     pl.BlockSpec((1, KV_W, tm), lambda i, j: (i, 0, j)), row(REC_W), row(REC_W)]
        out_shape = [jax.ShapeDtypeStruct((b, s // ATTN_BLK, ATTN_W, ATTN_BLK), BF16),
                     jax.ShapeDtypeStruct((b, s, KV_W), BF16),
                     jax.ShapeDtypeStruct((b, KV_W, s), BF16),
                     jax.ShapeDtypeStruct((b, s, REC_W), F32),
                     jax.ShapeDtypeStruct((b, s, REC_W), BF16)]
    return pl.pallas_call(
        functools.partial(_proj_kernel, meta=meta),
        grid=(b, s // tm),
        in_specs=[row(D_MODEL),
                  pl.BlockSpec((1, D_MODEL), lambda i, j: (0, 0)),
                  pl.BlockSpec((D_MODEL, IN_COLS), lambda i, j: (0, 0)),
                  tab_spec, tab_spec, tab_spec],
        out_specs=out_specs,
        out_shape=out_shape,
        compiler_params=_params("parallel", "parallel"),
        name="proj_meta" if meta else "proj",
    )(x, g, w_in, *tabs)


def _rope_tabs(pos0, n):
    half = ROT_DIM // 2
    inv = ROPE_THETA ** (-jnp.arange(0, ROT_DIM, 2, dtype=F32) / ROT_DIM)
    ang = jnp.arange(pos0 + n, dtype=F32)[pos0:, None] * inv[None, :]
    cos, sin = jnp.cos(ang), jnp.sin(ang)
    one = jnp.ones((n, HEAD_DIM - ROT_DIM), F32)
    zero = jnp.zeros((n, HEAD_DIM - ROT_DIM), F32)
    zh = jnp.zeros((n, half), F32)
    c = jnp.concatenate([cos, cos, one], axis=1)
    sa = jnp.concatenate([-sin, zh, zero], axis=1)
    sb = jnp.concatenate([zh, sin, zero], axis=1)
    return tuple(jnp.concatenate([t, t], axis=1) for t in (c, sa, sb))


def _attn_kernel(qt_ref, kp_ref, kc_ref, kn_ref, km_ref, vp_ref, vc_ref, vn_ref, vm_ref,
                 bias_ref, sink_ref, gt_ref, o_ref):
    blk = ATTN_BLK
    qt = qt_ref[0, 0]
    head = lambda h: qt[h * HEAD_DIM:(h + 1) * HEAD_DIM]
    zero = jnp.zeros((HEAD_DIM, Q_PER_KV * blk), BF16)
    w = jnp.concatenate(
        [jnp.concatenate([head(h) for h in range(Q_PER_KV)] + [zero], axis=1),
         jnp.concatenate([zero] + [head(h) for h in range(Q_PER_KV, N_HEADS)], axis=1)], axis=0)
    kall = jnp.concatenate([kp_ref[0], kc_ref[0], kn_ref[0], km_ref[...]], axis=0)
    vt = jnp.concatenate([vp_ref[0], vc_ref[0], vn_ref[0], vm_ref[...]], axis=1)
    nk = kall.shape[0]
    ones = jnp.ones((2 * SUBLANES, nk), BF16)
    lhs = [jnp.concatenate([vt[g * HEAD_DIM:(g + 1) * HEAD_DIM], ones], axis=0) for g in range(N_KV_HEADS)]
    bias = bias_ref[0]
    s = jnp.dot(kall, w, preferred_element_type=F32)
    s = s + jnp.concatenate([bias] * N_HEADS, axis=1)
    sk = sink_ref[...]
    m = jnp.maximum(jnp.max(s, axis=0, keepdims=True), sk)
    p = jnp.exp(s - m).astype(BF16)
    esk = jnp.exp(sk - m)
    outs = []
    for j in range(N_HEADS // 2):
        lanes = slice(j * 2 * blk, (j + 1) * 2 * blk)
        o = jnp.dot(lhs[(2 * j) // Q_PER_KV], p[:, lanes], preferred_element_type=F32)
        r = o[:HEAD_DIM] * (1.0 / (o[HEAD_DIM:HEAD_DIM + 1] + esk[:, lanes]))
        outs += [r[:, :blk], r[:, blk:]]
    at = jnp.concatenate(outs, axis=0)
    ms = jnp.mean(at * at, axis=0, keepdims=True)
    at = at * lax.rsqrt(ms + EPS) * gt_ref[...]
    o_ref[0] = at.T.astype(BF16)


def _attn_bias():
    nk = 3 * ATTN_BLK + N_META
    c = np.arange(nk)[:, None]
    r = np.arange(ATTN_BLK)[None, :]
    out = np.zeros((4, nk, ATTN_BLK), np.float32)
    for v in range(4):
        valid = np.ones((nk, ATTN_BLK), bool)
        prev = c < ATTN_BLK
        nxt = (c >= 2 * ATTN_BLK) & (c < 3 * ATTN_BLK)
        valid = np.where(prev, (c >= r) & (not v & 1), valid)
        valid = np.where(nxt, (c - 2 * ATTN_BLK <= r) & (not v & 2), valid)
        out[v] = np.where(valid, 0.0, NEG_BIG)
    return out


def _attn(qt, k, vt, km, vtm, sink_b, gt):
    b, nb = qt.shape[0], qt.shape[1]
    s = nb * ATTN_BLK
    nk = 3 * ATTN_BLK + N_META
    prev = lambda n: jnp.maximum(n - 1, 0)
    nxt = lambda n: jnp.minimum(n + 1, nb - 1)
    ks = lambda f: pl.BlockSpec((1, ATTN_BLK, KV_W), lambda i, n: (i, f(n), 0))
    vs = lambda f: pl.BlockSpec((1, KV_W, ATTN_BLK), lambda i, n: (i, 0, f(n)))
    same = lambda n: n
    edge = lambda i, n: (jnp.where(n == 0, 1, 0) + jnp.where(n == nb - 1, 2, 0), 0, 0)
    return pl.pallas_call(
        _attn_kernel,
        grid=(b, nb),
        in_specs=[pl.BlockSpec((1, 1, ATTN_W, ATTN_BLK), lambda i, n: (i, n, 0, 0)),
                  ks(prev), ks(same), ks(nxt), pl.BlockSpec((N_META, KV_W), lambda i, n: (0, 0)),
                  vs(prev), vs(same), vs(nxt), pl.BlockSpec((KV_W, N_META), lambda i, n: (0, 0)),
                  pl.BlockSpec((1, nk, ATTN_BLK), edge),
                  pl.BlockSpec((1, N_HEADS * ATTN_BLK), lambda i, n: (0, 0)),
                  pl.BlockSpec((ATTN_W, ATTN_BLK), lambda i, n: (0, 0))],
        out_specs=pl.BlockSpec((1, ATTN_BLK, ATTN_W), lambda i, n: (i, n, 0)),
        out_shape=jax.ShapeDtypeStruct((b, s, ATTN_W), BF16),
        compiler_params=_params("parallel", "parallel"),
        name="attn",
    )(qt, k, k, k, km, vt, vt, vt, vtm, jnp.asarray(_attn_bias()), sink_b, gt)


def _scan_rows(a, u, carry, reverse):
    rows, width = a.shape
    row = lax.broadcasted_iota(jnp.int32, (SUBLANES, width), 0)
    out = [None] * (rows // SUBLANES)
    order = range(rows // SUBLANES)
    for sidx in (reversed(order) if reverse else order):
        a8 = a[sidx * SUBLANES:(sidx + 1) * SUBLANES]
        u8 = u[sidx * SUBLANES:(sidx + 1) * SUBLANES]
        d = 1
        while d < SUBLANES:
            shift = SUBLANES - d if reverse else d
            keep = (row < SUBLANES - d) if reverse else (row >= d)
            a_s = pltpu.roll(a8, shift, 0)
            u_s = pltpu.roll(u8, shift, 0)
            u8 = jnp.where(keep, a8 * u_s + u8, u8)
            a8 = jnp.where(keep, a8 * a_s, a8)
            d *= 2
        h8 = a8 * carry + u8
        carry = h8[0:1] if reverse else h8[SUBLANES - 1:SUBLANES]
        out[sidx] = h8
    return jnp.concatenate(out, axis=0), carry


def _gate_scan(xc, w_ref, br, bi, nsp, carry, reverse):
    hs, cs = [], []
    half = REC_W // 2
    for ch in range(2):
        sl = slice(ch * half, (ch + 1) * half)
        xcc = xc[:, sl]
        pre = jnp.dot(xcc.astype(BF16), w_ref[ch], preferred_element_type=F32)
        t_r = jnp.tanh(pre[:, :half] + br[:, sl])
        t_i = jnp.tanh(pre[:, half:] + bi[:, sl])
        log_a = nsp[:, sl] * (t_r + 1.0)
        a = jnp.exp(log_a)
        z = jnp.tanh(log_a) * (a * a * (-0.25) + (-0.25))
        u = jnp.exp(0.5 * jnp.log(z)) * ((t_i + 1.0) * xcc)
        h, c = _scan_rows(a, u, carry[:, sl], reverse)
        hs.append(h)
        cs.append(c)
    return jnp.concatenate(hs, axis=1), jnp.concatenate(cs, axis=1)


def _rec_kernel(xp_ref, x_ref, xn_ref, xm_ref, cw_ref, cb_ref, w_ref, br_ref, bi_ref, nsp_ref,
                h_ref, xs_ref, carry_ref, *, tm, reverse):
    j = pl.program_id(1)
    nt = pl.num_programs(1)
    first_tile = (j == nt - 1) if reverse else (j == 0)
    last_tile = (j == 0) if reverse else (j == nt - 1)
    cw = cw_ref[...]
    cb = cb_ref[...]
    br, bi, nsp = br_ref[...], bi_ref[...], nsp_ref[...]
    halo = SUBLANES

    def conv(rows):
        acc = cb + cw[0:1] * xs_ref[halo - CONV_LEFT:halo - CONV_LEFT + rows]
        for t in range(1, CONV_W):
            acc = acc + cw[t:t + 1] * xs_ref[halo - CONV_LEFT + t:halo - CONV_LEFT + t + rows]
        return acc

    if reverse:
        @pl.when(j == 0)
        def _():
            carry_ref[...] = jnp.zeros_like(carry_ref)
    else:
        @pl.when(j == 0)
        def _():
            xs_ref[0:halo] = jnp.zeros((halo, REC_W), F32)
            xs_ref[halo:halo + N_META] = xm_ref[...]
            xs_ref[halo + N_META:2 * halo + N_META] = x_ref[0, 0:halo]
            _, c = _gate_scan(conv(N_META), w_ref, br, bi, nsp, jnp.zeros((1, REC_W), F32), False)
            carry_ref[0:1] = c

    xs_ref[0:halo] = jnp.where(first_tile, xm_ref[N_META - halo:N_META], xp_ref[0])
    xs_ref[halo:halo + tm] = x_ref[0]
    xs_ref[halo + tm:2 * halo + tm] = jnp.where(last_tile, jnp.zeros((halo, REC_W), F32), xn_ref[0])
    h, c = _gate_scan(conv(tm), w_ref, br, bi, nsp, carry_ref[0:1], reverse)
    h_ref[0] = h
    carry_ref[0:1] = c


def _rec(xr, xr_meta, cw, cb, wcat, br, bi, nsp, tm, reverse):
    b, s, _ = xr.shape
    nt = s // tm
    per = tm // SUBLANES
    nh = s // SUBLANES
    t_of = (lambda j: nt - 1 - j) if reverse else (lambda j: j)
    vec = lambda r: pl.BlockSpec((r, REC_W), lambda i, j: (0, 0))
    return pl.pallas_call(
        functools.partial(_rec_kernel, tm=tm, reverse=reverse),
        grid=(b, nt),
        in_specs=[pl.BlockSpec((1, SUBLANES, REC_W), lambda i, j: (i, jnp.maximum(t_of(j) * per - 1, 0), 0)),
                  pl.BlockSpec((1, tm, REC_W), lambda i, j: (i, t_of(j), 0)),
                  pl.BlockSpec((1, SUBLANES, REC_W),
                               lambda i, j: (i, jnp.minimum((t_of(j) + 1) * per, nh - 1), 0)),
                  vec(N_META), vec(CONV_W), vec(1),
                  pl.BlockSpec((2, REC_W // 2, REC_W), lambda i, j: (0, 0, 0)),
                  vec(1), vec(1), vec(1)],
        out_specs=pl.BlockSpec((1, tm, REC_W), lambda i, j: (i, t_of(j), 0)),
        out_shape=jax.ShapeDtypeStruct((b, s, REC_W), F32),
        scratch_shapes=[pltpu.VMEM((tm + 2 * SUBLANES, REC_W), F32), pltpu.VMEM((SUBLANES, REC_W), F32)],
        compiler_params=_params("parallel", "arbitrary"),
        name="rec_bwd" if reverse else "rec_fwd",
    )(xr, xr, xr, xr_meta, cw, cb, wcat, br, bi, nsp)


def _gate_weights(w_r, w_i):
    per = REC_BLOCKS // 2

    def bd(w4):
        z = jnp.zeros((per * REC_BW, per * REC_BW), F32)
        for n in range(per):
            z = z.at[n * REC_BW:(n + 1) * REC_BW, n * REC_BW:(n + 1) * REC_BW].set(w4[n])
        return z

    halves = [jnp.concatenate([bd(w_r[c * per:(c + 1) * per]), bd(w_i[c * per:(c + 1) * per])], axis=1)
              for c in range(2)]
    return (0.5 * jnp.stack(halves)).astype(BF16)


def _pair_index(a, b):
    return (a * (2 * EXP_PER_GROUP - 1 - a)) // 2 + (b - a - 1)


def _outproj_kernel(at_ref, hf_ref, hb_ref, yg_ref, x_ref, w_ref, grec_ref, gffn_ref, wr_ref, brt_ref, o_ref):
    yg = yg_ref[0].astype(F32)
    gelu = 0.5 * yg * (1.0 + jnp.tanh(np.sqrt(2.0 / np.pi).astype(np.float32) * (yg + 0.044715 * (yg * yg * yg))))
    rec = (hf_ref[0] + hb_ref[0]) * gelu
    mixed = jnp.concatenate([at_ref[0], _rms(rec, grec_ref[...]).astype(BF16)], axis=1)
    h1 = x_ref[0] + jnp.dot(mixed, w_ref[...], preferred_element_type=F32)
    o_ref[0, :, :D_MODEL] = h1

    u = _rms(h1, gffn_ref[...]).astype(BF16)
    lg = jnp.dot(u, wr_ref[...], preferred_element_type=F32) + brt_ref[...]
    lane_i = lax.broadcasted_iota(jnp.int32, lg.shape, 1)
    lane = lane_i.astype(F32)
    lane_grp = (lane_i >> 3).astype(F32)
    ninf = -jnp.inf
    big = jnp.float32(1 << 20)
    is_grp = (lane_i >= N_EXPERTS) & (lane_i < N_EXPERTS + N_GROUPS)
    gl = jnp.where(is_grp, lg, ninf)
    gmax = jnp.max(gl, axis=-1, keepdims=True)
    gidx = jnp.min(jnp.where(gl == gmax, lane - N_EXPERTS, big), axis=-1, keepdims=True)
    p_grp = 1.0 / jnp.sum(jnp.where(is_grp, jnp.exp(gl - gmax), 0.0), axis=-1, keepdims=True)
    in_grp = (lane_i < N_EXPERTS) & (lane_grp == gidx)
    el = jnp.where(in_grp, lg, ninf)
    m1 = jnp.max(el, axis=-1, keepdims=True)
    i1 = jnp.min(jnp.where(el == m1, lane, big), axis=-1, keepdims=True)
    el2 = jnp.where(lane == i1, ninf, el)
    m2 = jnp.max(el2, axis=-1, keepdims=True)
    i2 = jnp.min(jnp.where(el2 == m2, lane, big), axis=-1, keepdims=True)
    e2 = jnp.exp(m2 - m1)
    g1 = p_grp / (1.0 + e2)
    g2 = p_grp * e2 / (1.0 + e2)
    first_lo = i1 < i2
    lo = jnp.where(first_lo, i1, i2) - gidx * EXP_PER_GROUP
    hi = jnp.where(first_lo, i2, i1) - gidx * EXP_PER_GROUP
    cls = gidx * PAIRS_PER_GROUP + 0.5 * (lo * (2 * EXP_PER_GROUP - 1 - lo)) + (hi - lo - 1.0)
    g_lo = jnp.where(first_lo, g1, g2)
    g_hi = jnp.where(first_lo, g2, g1)
    tl = lax.broadcasted_iota(jnp.int32, (lg.shape[0], LANES), 1)
    o_ref[0, :, D_MODEL:] = jnp.where(tl == 0, g_lo, jnp.where(tl == 1, g_hi, jnp.where(tl == 2, cls, 0.0)))


def _outproj(attn_n, hf, hb, yg, x, w_out, g_rec, g_ffn, w_router, b_router, tm):
    b, s, _ = x.shape
    row = lambda w: pl.BlockSpec((1, tm, w), lambda i, j: (i, j, 0))
    full = lambda r, c: pl.BlockSpec((r, c), lambda i, j: (0, 0))
    return pl.pallas_call(
        _outproj_kernel,
        grid=(b, s // tm),
        in_specs=[row(ATTN_W), row(REC_W), row(REC_W), row(REC_W), row(D_MODEL),
                  full(D_MODEL, D_MODEL), full(1, REC_W), full(1, D_MODEL),
                  full(D_MODEL, LANES), full(1, LANES)],
        out_specs=row(ROW_W),
        out_shape=jax.ShapeDtypeStruct((b, s, ROW_W), F32),
        compiler_params=_params("parallel", "parallel"),
        name="outproj",
    )(attn_n, hf, hb, yg, x, w_out, g_rec, g_ffn, w_router, b_router)


def _class_tables():
    lo = np.zeros((N_CLASSES,), np.int32)
    hi = np.zeros((N_CLASSES,), np.int32)
    for g in range(N_GROUPS):
        for a in range(EXP_PER_GROUP):
            for b in range(a + 1, EXP_PER_GROUP):
                c = g * PAIRS_PER_GROUP + _pair_index(a, b)
                lo[c] = g * EXP_PER_GROUP + a
                hi[c] = g * EXP_PER_GROUP + b
    return lo, hi


def _route_plan(cls, bm):
    n = cls.shape[0]
    nblk = n // bm + N_CLASSES
    i32 = jnp.int32
    order = jnp.argsort(cls, stable=True).astype(i32)
    counts = jnp.sum((cls[:, None] == jnp.arange(N_CLASSES, dtype=i32)[None, :]).astype(i32), axis=0)
    start = jnp.cumsum(counts) - counts
    padded = (counts + bm - 1) // bm * bm
    pend = jnp.cumsum(padded)
    pstart = pend - padded
    brow = jnp.arange(nblk, dtype=i32) * bm
    blk_cls = jnp.minimum(jnp.searchsorted(pend, brow, side='right').astype(i32), N_CLASSES - 1)
    nval = jnp.clip(counts[blk_cls] - (brow - pstart[blk_cls]), 0, bm)
    nval = jnp.where(brow < pend[-1], nval, 0).astype(i32)
    idx = jnp.arange(bm, dtype=i32)[None, :] + (brow - pstart[blk_cls] + start[blk_cls])[:, None]
    tok = order[jnp.clip(idx, 0, n - 1)]
    row_tok = jnp.where(jnp.arange(bm, dtype=i32)[None, :] < nval[:, None], tok, 0).astype(i32)
    lo_t, hi_t = _class_tables()
    return row_tok.reshape(nblk, 1, bm), jnp.asarray(lo_t)[blk_cls], jnp.asarray(hi_t)[blk_cls], nval


def _moe_kernel(lo_ref, hi_ref, nval_ref, rt_ref, rtn_ref, h_hbm,
                wg_lo, wu_lo, wd_lo, wg_hi, wu_hi, wd_hi, gffn_ref, gfin_ref,
                out_hbm, xbuf, ybuf, gsem, ssem):
    b = pl.program_id(0)
    nb = pl.num_programs(0)
    slot = b & 1
    nv = nval_ref[b]

    def gather_copy(tok, r, s):
        return pltpu.make_async_copy(h_hbm.at[pl.ds(tok, 1)], xbuf.at[s, pl.ds(r, 1)], gsem.at[s])

    def scatter_copy(tok, r, s):
        return pltpu.make_async_copy(ybuf.at[s, pl.ds(r, 1)], out_hbm.at[pl.ds(tok, 1)], ssem.at[s])

    bm = xbuf.shape[1]

    def for_rows(count, fn):
        @pl.when(count == bm)
        def _():
            for r in range(bm):
                fn(r)

        @pl.when(count < bm)
        def _():
            def body(r, carry):
                fn(r)
                return carry
            lax.fori_loop(0, count, body, 0)

    def gather_start(rt, count, s):
        for_rows(count, lambda r: gather_copy(rt[0, 0, r], r, s).start())

    def gather_wait(count, s):
        for_rows(count, lambda r: gather_copy(0, 0, s).wait())

    def scatter_start(rt, count, s):
        for_rows(count, lambda r: scatter_copy(rt[0, 0, r], r, s).start())

    def scatter_wait(count, s):
        for_rows(count, lambda r: scatter_copy(0, 0, s).wait())

    @pl.when(b == 0)
    def _():
        xbuf[...] = jnp.zeros_like(xbuf)
        gather_start(rt_ref, nv, 0)

    gather_wait(nv, slot)

    @pl.when(b + 1 < nb)
    def _():
        gather_start(rtn_ref, nval_ref[jnp.minimum(b + 1, nb - 1)], 1 - slot)

    @pl.when(b >= 2)
    def _():
        scatter_wait(nval_ref[jnp.maximum(b - 2, 0)], slot)

    @pl.when(nv > 0)
    def _():
        x = xbuf[slot]
        h1 = x[:, :D_MODEL]
        g_lo = x[:, D_MODEL:D_MODEL + 1]
        g_hi = x[:, D_MODEL + 1:D_MODEL + 2]
        u = _rms(h1, gffn_ref[...]).astype(BF16)

        def expert(wg, wu, wd):
            a = jnp.dot(u, wg[0], preferred_element_type=F32)
            c = jnp.dot(u, wu[0], preferred_element_type=F32)
            hid = (a * _sigmoid(a)) * c
            return jnp.dot(hid.astype(BF16), wd[0], preferred_element_type=F32)

        y = g_lo * expert(wg_lo, wu_lo, wd_lo) + g_hi * expert(wg_hi, wu_hi, wd_hi)
        ybuf[slot] = _rms(h1 + y, gfin_ref[...])

    scatter_start(rt_ref, nv, slot)

    @pl.when(b == nb - 1)
    def _():
        scatter_wait(nv, slot)

        @pl.when(b >= 1)
        def _():
            scatter_wait(nval_ref[jnp.maximum(b - 1, 0)], 1 - slot)


def _moe(rows, row_tok, blk_lo, blk_hi, nval, wg, wu, wd, g_ffn, g_fin, bm):
    n = rows.shape[0]
    nblk = row_tok.shape[0]
    w_in = lambda tab: pl.BlockSpec((1, D_MODEL, D_EXPERT), lambda i, lo, hi, nv: ((lo, hi)[tab][i], 0, 0))
    w_dn = lambda tab: pl.BlockSpec((1, D_EXPERT, D_MODEL), lambda i, lo, hi, nv: ((lo, hi)[tab][i], 0, 0))
    vec = pl.BlockSpec((1, D_MODEL), lambda i, lo, hi, nv: (0, 0))
    grid_spec = pltpu.PrefetchScalarGridSpec(
        num_scalar_prefetch=3,
        grid=(nblk,),
        in_specs=[pl.BlockSpec((1, 1, bm), lambda i, lo, hi, nv: (i, 0, 0), memory_space=pltpu.SMEM),
                  pl.BlockSpec((1, 1, bm), lambda i, lo, hi, nv: (jnp.minimum(i + 1, nblk - 1), 0, 0),
                               memory_space=pltpu.SMEM),
                  pl.BlockSpec(memory_space=pl.ANY),
                  w_in(0), w_in(0), w_dn(0), w_in(1), w_in(1), w_dn(1), vec, vec],
        out_specs=pl.BlockSpec(memory_space=pl.ANY),
        scratch_shapes=[pltpu.VMEM((2, bm, ROW_W), F32), pltpu.VMEM((2, bm, D_MODEL), F32),
                        pltpu.SemaphoreType.DMA((2,)), pltpu.SemaphoreType.DMA((2,))],
    )
    return pl.pallas_call(
        _moe_kernel,
        grid_spec=grid_spec,
        out_shape=jax.ShapeDtypeStruct((n, D_MODEL), F32),
        compiler_params=_params("arbitrary"),
        name="moe",
    )(blk_lo, blk_hi, nval, row_tok, row_tok, rows, wg, wu, wd, wg, wu, wd, g_ffn, g_fin)


def _tile(s, pref):
    t = min(pref, s)
    while s % t:
        t -= ATTN_BLK
    return t


def _encode(x, meta_tokens, p, moe_bm=128):
    b, s, _ = x.shape
    assert s % ATTN_BLK == 0
    tm = _tile(s, 512)
    qt, k, vt, xr, yg = _proj(x, p['ln_mix_g'], p['w_in'], _rope_tabs(N_META, s), tm)
    km, vm, xrm = _proj(meta_tokens[None], p['ln_mix_g'], p['w_in'], _rope_tabs(0, N_META), N_META, meta=True)
    attn_n = _attn(qt, k, vt, km[0], vm[0].T, p['attn_sink'], p['g_attn_out'])
    tr = _tile(s, 256)
    rec_args = (xr, xrm[0], p['conv_w'], p['conv_b'])
    hf = _rec(*rec_args, p['wcat_fwd'], p['b_rgate_fwd'], p['b_igate_fwd'], p['nsp_fwd'], tr, False)
    hb = _rec(*rec_args, p['wcat_bwd'], p['b_rgate_bwd'], p['b_igate_bwd'], p['nsp_bwd'], tr, True)
    rows = _outproj(attn_n, hf, hb, yg, x, p['w_out'], p['g_rec_out'], p['ln_ffn_g'],
                    p['w_router'], p['b_router'], tm)
    rows = rows.reshape(b * s, ROW_W)
    cls = rows[:, D_MODEL + 2].astype(jnp.int32)
    row_tok, blk_lo, blk_hi, nval = _route_plan(cls, moe_bm)
    out = _moe(rows, row_tok, blk_lo, blk_hi, nval, p['w_exp_gate'], p['w_exp_up'], p['w_exp_down'],
               p['ln_ffn_g'], p['ln_final_g'], moe_bm)
    return out.reshape(b, s, D_MODEL)


def _prepare(ln_mix_g, w_in, conv_w, conv_b, w_rgate_fwd, b_rgate_fwd, w_igate_fwd, b_igate_fwd, lam_fwd,
             w_rgate_bwd, b_rgate_bwd, w_igate_bwd, b_igate_bwd, lam_bwd, attn_sink, g_attn_out, g_rec_out,
             w_out, ln_ffn_g, w_router_grp, b_router_grp, w_router_exp, b_router_exp,
             w_exp_gate, w_exp_up, w_exp_down, ln_final_g):
    row = lambda a: a.reshape(1, -1).astype(F32)
    pad = LANES - N_EXPERTS - N_GROUPS
    w_router = jnp.concatenate([w_router_exp[0], w_router_grp[0], jnp.zeros((D_MODEL, pad), F32)], axis=1)
    b_router = jnp.concatenate([b_router_exp[0], b_router_grp[0], jnp.zeros((pad,), F32)])
    return {
        'ln_mix_g': row(ln_mix_g[0]), 'w_in': w_in[0].astype(BF16),
        'conv_w': conv_w[0].astype(F32), 'conv_b': row(conv_b[0]),
        'wcat_fwd': _gate_weights(w_rgate_fwd[0], w_igate_fwd[0]),
        'wcat_bwd': _gate_weights(w_rgate_bwd[0], w_igate_bwd[0]),
        'b_rgate_fwd': row(0.5 * b_rgate_fwd[0]), 'b_igate_fwd': row(0.5 * b_igate_fwd[0]),
        'b_rgate_bwd': row(0.5 * b_rgate_bwd[0]), 'b_igate_bwd': row(0.5 * b_igate_bwd[0]),
        'nsp_fwd': row(0.5 * (-RG_C * jax.nn.softplus(-lam_fwd[0]))),
        'nsp_bwd': row(0.5 * (-RG_C * jax.nn.softplus(-lam_bwd[0]))),
        'attn_sink': row(jnp.repeat(attn_sink[0], ATTN_BLK)),
        'g_attn_out': jnp.broadcast_to(g_attn_out[0].astype(F32)[:, None], (ATTN_W, ATTN_BLK)),
        'g_rec_out': row(g_rec_out[0]),
        'w_out': w_out[0].astype(BF16), 'ln_ffn_g': row(ln_ffn_g[0]),
        'w_router': w_router.astype(BF16), 'b_router': row(b_router),
        'w_exp_gate': w_exp_gate[0].astype(BF16), 'w_exp_up': w_exp_up[0].astype(BF16),
        'w_exp_down': w_exp_down[0].astype(BF16), 'ln_final_g': row(ln_final_g),
    }


def kernel(x_prompt, x_sample, meta_tokens, ln_mix_g, w_in, conv_w, conv_b, w_rgate_fwd, b_rgate_fwd, w_igate_fwd, b_igate_fwd, lam_fwd, w_rgate_bwd, b_rgate_bwd, w_igate_bwd, b_igate_bwd, lam_bwd, attn_sink, g_attn_out, g_rec_out, w_out, ln_ffn_g, w_router_grp, b_router_grp, w_router_exp, b_router_exp, w_exp_gate, w_exp_up, w_exp_down, ln_final_g):
    p = _prepare(ln_mix_g, w_in, conv_w, conv_b, w_rgate_fwd, b_rgate_fwd, w_igate_fwd, b_igate_fwd, lam_fwd,
                 w_rgate_bwd, b_rgate_bwd, w_igate_bwd, b_igate_bwd, lam_bwd, attn_sink, g_attn_out, g_rec_out,
                 w_out, ln_ffn_g, w_router_grp, b_router_grp, w_router_exp, b_router_exp,
                 w_exp_gate, w_exp_up, w_exp_down, ln_final_g)
    return (_encode(x_prompt, meta_tokens, p), _encode(x_sample, meta_tokens, p))
```
